```python
import math
import jax
import jax.numpy as jnp
from jax import lax
import numpy as np

D_MODEL = 1024
BATCH = 4
SEQ = 4096
DEPTH = 2

D_MIX = D_MODEL
ATT_QK_DIM = 64
ATT_V_DIM = 2 * ATT_QK_DIM
ATT_WIDTH = D_MIX // 2
ATT_HEADS = ATT_WIDTH // ATT_V_DIM
HG_DK = 64
HG_DV = 64
HG_WIDTH = D_MIX // 4
HG_HEADS = HG_WIDTH // HG_DV
DN_DK = 64
DN_DV = 64
DN_WIDTH = D_MIX - ATT_WIDTH - HG_WIDTH
DN_HEADS = DN_WIDTH // DN_DV
DN_CONV = 4
DN_CONV_CH = 2 * DN_HEADS * DN_DK + DN_HEADS * DN_DV
FFN_CONV = 3
D_FF = ((8 * D_MODEL // 3 + 255) // 256) * 256
QUERY_BLOCK = 128
HG_CHUNK = 16
DN_CHUNK = 64
EPS = 1e-6
MASK_NEG = -1e30
F_FLOOR = 1e-30

IN_SIZES = (
    ATT_HEADS * 2 * ATT_QK_DIM,
    ATT_HEADS * 2 * ATT_QK_DIM,
    ATT_HEADS * ATT_V_DIM,
    HG_HEADS * HG_DK,
    HG_HEADS * HG_DK,
    HG_HEADS * HG_DV,
    HG_WIDTH,
    DN_HEADS * DN_DK,
    DN_HEADS * DN_DK,
    DN_HEADS * DN_DV,
    DN_WIDTH,
    DN_HEADS,
    DN_HEADS,
)
IN_OFFSETS = tuple(int(o) for o in np.cumsum(IN_SIZES)[:-1])
D_IN = int(sum(IN_SIZES))

kernel_name = "hybrid_diffattn_hgrn2_gdn_convffn"


def rmsnorm(x, w):
    xf = x.astype(jnp.float32)
    y = xf * lax.rsqrt(jnp.mean(xf * xf, axis=-1, keepdims=True) + EPS)
    return (y * w.astype(jnp.float32)).astype(x.dtype)


def l2norm(x):
    return x * lax.rsqrt(jnp.sum(x * x, axis=-1, keepdims=True) + EPS)


def causal_depthwise_conv(x, w):
    K, C = w.shape
    return lax.conv_general_dilated(
        x, w[:, None, :].astype(x.dtype), window_strides=(1,), padding=[(K - 1, 0)],
        dimension_numbers=("NWC", "WIO", "NWC"), feature_group_count=C)


def diff_attention(q, k, v, lam, subln_w, lambda_init):
    B, T = q.shape[:2]
    nb = T // QUERY_BLOCK
    scale = ATT_QK_DIM ** -0.5
    qb = q.reshape(B, nb, QUERY_BLOCK, ATT_HEADS, 2, ATT_QK_DIM).transpose(1, 0, 3, 4, 2, 5)
    kt = k.transpose(0, 2, 3, 1, 4)
    vt = v.transpose(0, 2, 1, 3)
    kpos = jnp.arange(T)

    def block(args):
        qi, i = args
        s = jnp.einsum("bhmqd,bhmkd->bhmqk", qi, kt).astype(jnp.float32) * scale
        qpos = i * QUERY_BLOCK + jnp.arange(QUERY_BLOCK)
        s = jnp.where(kpos[None, :] <= qpos[:, None], s, MASK_NEG)
        p = jax.nn.softmax(s, axis=-1)
        pd = p[:, :, 0] - lam * p[:, :, 1]
        return jnp.einsum("bhqk,bhkv->bhqv", pd.astype(vt.dtype), vt)

    o = lax.map(block, (qb, jnp.arange(nb)))
    o = o.transpose(1, 0, 3, 2, 4).reshape(B, T, ATT_HEADS, ATT_V_DIM)
    o = rmsnorm(o, subln_w) * (1.0 - lambda_init)
    return o.reshape(B, T, ATT_WIDTH)


def chunked_gla(q, k, v, log_f):
    B, H, T, dk = q.shape
    dv = v.shape[-1]
    C = HG_CHUNK
    N = T // C
    q, k, log_f = (t.reshape(B, H, N, C, dk) for t in (q, k, log_f))
    v = v.reshape(B, H, N, C, dv)
    b = jnp.cumsum(log_f, axis=3)
    causal = jnp.tril(jnp.ones((C, C), dtype=bool))
    rel = jnp.where(causal[:, :, None], b[:, :, :, :, None, :] - b[:, :, :, None, :, :], MASK_NEG)
    a = jnp.einsum("bhntd,bhnsd,bhntsd->bhnts", q, k, jnp.exp(rel))
    o_intra = jnp.einsum("bhnts,bhnsv->bhntv", a, v)
    b_last = b[:, :, :, -1]
    kv = jnp.einsum("bhncd,bhncv->bhndv", k * jnp.exp(b_last[:, :, :, None] - b), v)

    def step(S, xs):
        decay, kv_n = xs
        return decay[..., None] * S + kv_n, S

    _, S_prev = lax.scan(step, jnp.zeros((B, H, dk, dv), jnp.float32),
                         (jnp.exp(b_last).transpose(2, 0, 1, 3), kv.transpose(2, 0, 1, 3, 4)))
    o_inter = jnp.einsum("bhncd,nbhdv->bhncv", q * jnp.exp(b), S_prev)
    return (o_intra + o_inter).reshape(B, H, T, dv)


def hgrn2_mixer(q, f_pre, i, g, lb, norm_w):
    B, T, _ = q.shape
    dt = q.dtype
    qf = jax.nn.silu(q.astype(jnp.float32))
    lbf = lb.astype(jnp.float32)
    fp = f_pre.astype(jnp.float32)
    f = lbf + (1.0 - lbf) * jax.nn.sigmoid(fp)
    log_f = jnp.log(jnp.maximum(f, F_FLOOR))
    kf = (1.0 - lbf) * jax.nn.sigmoid(-fp)

    def heads(t, d):
        return t.reshape(B, T, HG_HEADS, d).transpose(0, 2, 1, 3)

    o = chunked_gla(heads(qf, HG_DK), heads(kf, HG_DK), heads(i.astype(jnp.float32), HG_DV), heads(log_f, HG_DK))
    o = o.transpose(0, 2, 1, 3)
    o = rmsnorm(o, norm_w) * jax.nn.silu(g.astype(jnp.float32).reshape(B, T, HG_HEADS, HG_DV))
    return o.reshape(B, T, HG_WIDTH).astype(dt)


def chunked_gated_delta(q, k, v, g, beta):
    B, H, T, dk = q.shape
    dv = v.shape[-1]
    C = DN_CHUNK
    N = T // C
    q = q.reshape(B, H, N, C, dk)
    k = k.reshape(B, H, N, C, dk)
    v = v.reshape(B, H, N, C, dv)
    g = g.reshape(B, H, N, C)
    beta = beta.reshape(B, H, N, C)
    gc = jnp.cumsum(g, axis=-1)
    incl = jnp.tril(jnp.ones((C, C), dtype=bool))
    strict = jnp.tril(jnp.ones((C, C), dtype=bool), k=-1)
    decay = jnp.exp(jnp.where(incl, gc[..., :, None] - gc[..., None, :], MASK_NEG))
    kb = k * beta[..., None]
    a_kk = jnp.where(strict, jnp.einsum("bhncd,bhnsd->bhncs", kb, k) * decay, 0.0)
    rhs = jnp.concatenate([v * beta[..., None], kb * jnp.exp(gc)[..., None]], axis=-1)
    sol = lax.linalg.triangular_solve(a_kk + jnp.eye(C, dtype=jnp.float32), rhs,
                                      left_side=True, lower=True, unit_diagonal=True)
    u, w = sol[..., :dv], sol[..., dv:]
    a_qk = jnp.einsum("bhncd,bhnsd->bhncs", q, k) * decay
    q_dec = q * jnp.exp(gc)[..., None]
    k_dec = k * jnp.exp(gc[..., -1:] - gc)[..., None]
    g_last = jnp.exp(gc[..., -1])

    def step(S, xs):
        u_n, w_n, aqk_n, qd_n, kd_n, gl_n = xs
        v_new = u_n - jnp.einsum("bhcd,bhdv->bhcv", w_n, S)
        o = jnp.einsum("bhcd,bhdv->bhcv", qd_n, S) + jnp.einsum("bhcs,bhsv->bhcv", aqk_n, v_new)
        S = gl_n[..., None, None] * S + jnp.einsum("bhcd,bhcv->bhdv", kd_n, v_new)
        return S, o

    xs = tuple(jnp.moveaxis(t, 2, 0) for t in (u, w, a_qk, q_dec, k_dec, g_last))
    _, o = lax.scan(step, jnp.zeros((B, H, dk, dv), jnp.float32), xs)
    return jnp.moveaxis(o, 0, 2).reshape(B, H, T, dv)


def gated_deltanet_mixer(q, k, v, z, a, b, conv_w, A_log, dt_bias, norm_w):
    B, T, _ = q.shape
    dt = q.dtype
    qkv = jax.nn.silu(causal_depthwise_conv(jnp.concatenate([q, k, v], axis=-1), conv_w).astype(jnp.float32))
    qc, kc, vc = jnp.split(qkv, [DN_HEADS * DN_DK, 2 * DN_HEADS * DN_DK], axis=-1)

    def heads(t, d):
        return t.reshape(B, T, DN_HEADS, d).transpose(0, 2, 1, 3)

    qh = l2norm(heads(qc, DN_DK)) * (DN_DK ** -0.5)
    kh = l2norm(heads(kc, DN_DK))
    vh = heads(vc, DN_DV)
    g = -jnp.exp(A_log.astype(jnp.float32)) * jax.nn.softplus(a.astype(jnp.float32) + dt_bias.astype(jnp.float32))
    beta = jax.nn.sigmoid(b.astype(jnp.float32))
    o = chunked_gated_delta(qh, kh, vh, g.transpose(0, 2, 1), beta.transpose(0, 2, 1))
    o = o.transpose(0, 2, 1, 3)
    o = rmsnorm(o, norm_w) * jax.nn.silu(z.astype(jnp.float32).reshape(B, T, DN_HEADS, DN_DV))
    return o.reshape(B, T, DN_WIDTH).astype(dt)


def setup_inputs(seed: int = 0) -> dict:
    key = jax.random.key(seed)
    ks = jax.random.split(key, 20)
    f32 = jnp.float32

    def normal(k, shape, scale):
        return jax.random.normal(k, shape, f32) * scale

    dt0 = jnp.exp(jax.random.uniform(ks[9], (DEPTH, DN_HEADS), f32, math.log(1e-3), math.log(1e-1)))
    return {
        "x": normal(ks[0], (BATCH, SEQ, D_MODEL), 1.0),
        "attn_norm_w": 1.0 + normal(ks[1], (DEPTH, D_MODEL), 0.02),
        "w_in": normal(ks[2], (DEPTH, D_MODEL, D_IN), D_MODEL ** -0.5),
        "diff_lambda": normal(ks[3], (DEPTH, 4, ATT_QK_DIM), 0.1),
        "diff_subln_w": 1.0 + normal(ks[4], (DEPTH, ATT_V_DIM), 0.02),
        "hgrn_lb_logits": normal(ks[5], (DEPTH, HG_HEADS * HG_DK), 0.1),
        "hgrn_norm_w": 1.0 + normal(ks[6], (DEPTH, HG_DV), 0.02),
        "dn_conv_w": normal(ks[7], (DEPTH, DN_CONV, DN_CONV_CH), DN_CONV ** -0.5),
        "dn_A_log": jnp.log(jax.random.uniform(ks[8], (DEPTH, DN_HEADS), f32, 1.0, 16.0)),
        "dn_dt_bias": dt0 + jnp.log(-jnp.expm1(-dt0)),
        "dn_norm_w": 1.0 + normal(ks[10], (DEPTH, DN_DV), 0.02),
        "w_out": normal(ks[11], (DEPTH, D_MIX, D_MODEL), D_MIX ** -0.5),
        "ffn_norm_w": 1.0 + normal(ks[12], (DEPTH, D_MODEL), 0.02),
        "ffn_w_up": normal(ks[13], (DEPTH, D_MODEL, 2 * D_FF), D_MODEL ** -0.5),
        "ffn_conv_w": normal(ks[14], (DEPTH, FFN_CONV, 2 * D_FF), FFN_CONV ** -0.5),
        "ffn_conv_b": normal(ks[15], (DEPTH, 2 * D_FF), 0.02),
        "ffn_w_down": normal(ks[16], (DEPTH, D_FF, D_MODEL), D_FF ** -0.5),
        "final_norm_w": 1.0 + normal(ks[17], (D_MODEL,), 0.02),
    }


def reference(x, attn_norm_w, w_in, diff_lambda, diff_subln_w, hgrn_lb_logits, hgrn_norm_w,
              dn_conv_w, dn_A_log, dn_dt_bias, dn_norm_w, w_out, ffn_norm_w, ffn_w_up,
              ffn_conv_w, ffn_conv_b, ffn_w_down, final_norm_w):
    lb_soft = jax.nn.softmax(hgrn_lb_logits.astype(jnp.float32), axis=0)
    lower_bounds = jnp.cumsum(lb_soft, axis=0) - lb_soft[0]
    B, T, _ = x.shape
    h = x
    for l in range(DEPTH):
        u = rmsnorm(h, attn_norm_w[l])
        proj = jnp.einsum("btd,de->bte", u, w_in[l])
        (aq, ak, av, hq, hf, hi, hg, dq, dk, dv, dz, da, db) = jnp.split(proj, IN_OFFSETS, axis=-1)

        lambda_init = 0.8 - 0.6 * math.exp(-0.3 * l)
        lam_p = diff_lambda[l].astype(jnp.float32)
        lam = jnp.exp(jnp.sum(lam_p[0] * lam_p[1])) - jnp.exp(jnp.sum(lam_p[2] * lam_p[3])) + lambda_init
        att_o = diff_attention(aq.reshape(B, T, ATT_HEADS, 2, ATT_QK_DIM),
                               ak.reshape(B, T, ATT_HEADS, 2, ATT_QK_DIM),
                               av.reshape(B, T, ATT_HEADS, ATT_V_DIM),
                               lam, diff_subln_w[l], lambda_init)
        hg_o = hgrn2_mixer(hq, hf, hi, hg, lower_bounds[l], hgrn_norm_w[l])
        dn_o = gated_deltanet_mixer(dq, dk, dv, dz, da, db, dn_conv_w[l], dn_A_log[l], dn_dt_bias[l], dn_norm_w[l])
        mix = jnp.concatenate([att_o, hg_o, dn_o], axis=-1)
        h = h + jnp.einsum("bte,ed->btd", mix, w_out[l])

        u = rmsnorm(h, ffn_norm_w[l])
        up = jnp.einsum("btd,df->btf", u, ffn_w_up[l])
        up = causal_depthwise_conv(up, ffn_conv_w[l]) + ffn_conv_b[l].astype(up.dtype)
        gate, val = jnp.split(up, 2, axis=-1)
        h = h + jnp.einsum("btf,fd->btd", jax.nn.silu(gate) * val, ffn_w_down[l])
    return rmsnorm(h, final_norm_w)
```

```python
import functools
import math

import jax
import jax.numpy as jnp
from jax import lax
from jax.experimental import pallas as pl
from jax.experimental.pallas import tpu as pltpu

F32 = jnp.float32
BF16 = jnp.bfloat16
HIGHEST = lax.Precision.HIGHEST

D_MODEL = 1024
DEPTH = 2
ATT_QK_DIM = 64
ATT_V_DIM = 128
ATT_HEADS = 4
ATT_WIDTH = 512
HEADS = 4
HEAD_DIM = 64
REC_WIDTH = HEADS * HEAD_DIM
DN_CONV = 4
DN_CONV_CH = 3 * REC_WIDTH
FFN_CONV = 3
D_FF = 2816
HG_CHUNK = 16
DN_CHUNK = 64
EPS = 1e-6
MASK_NEG = -1e30
F_FLOOR = 1e-30

ATT_COLS = 3 * ATT_WIDTH
HG_COLS = 4 * REC_WIDTH
DN_COLS = 4 * REC_WIDTH
AB_COLS = 128
REST_COLS = HG_COLS + DN_COLS + AB_COLS
D_IN = ATT_COLS + HG_COLS + DN_COLS + 2 * HEADS
D_IN_PAD = ATT_COLS + REST_COLS

VMEM_LIMIT = 56 * 1024 * 1024
LANES = 128
HALO = 16


def _cparams(sem):
    return pltpu.CompilerParams(dimension_semantics=sem, vmem_limit_bytes=VMEM_LIMIT)


def _rms(x, w):
    return x * lax.rsqrt(jnp.mean(x * x, axis=-1, keepdims=True) + EPS) * w


def _sigmoid(x):
    return 1.0 / (1.0 + jnp.exp(-x))


def _silu(x):
    return x * _sigmoid(x)


def _dot(a, b, precision=None):
    return jnp.dot(a, b, preferred_element_type=F32, precision=precision)


def _head_of(idx):
    return lax.shift_right_logical(idx, 6)


def _same_head(shape):
    r = lax.broadcasted_iota(jnp.int32, shape, 0)
    c = lax.broadcasted_iota(jnp.int32, shape, 1)
    return _head_of(r) == _head_of(c)


def _block_diag(x, mask):
    return jnp.where(mask, jnp.concatenate([x] * HEADS, axis=0), 0.0).astype(BF16)


def _chunk_cumsum_mats(rows, chunk_shift):
    r = lax.broadcasted_iota(jnp.int32, (rows, rows), 0)
    c = lax.broadcasted_iota(jnp.int32, (rows, rows), 1)
    same = lax.shift_right_logical(r, chunk_shift) == lax.shift_right_logical(c, chunk_shift)
    incl = jnp.where(same & (c <= r), 1.0, 0.0).astype(F32)
    total = jnp.where(same, 1.0, 0.0).astype(F32)
    return incl, total


def _inproj_body(x_ref, nw_ref, w_ref, att_ref, rest_ref):
    u = _rms(x_ref[...], nw_ref[...]).astype(BF16)
    step = 512
    for c0 in range(0, ATT_COLS, step):
        att_ref[:, c0:c0 + step] = _dot(u, w_ref[:, c0:c0 + step]).astype(BF16)
    for c0 in range(0, REST_COLS, step):
        cw = min(step, REST_COLS - c0)
        rest_ref[:, c0:c0 + cw] = _dot(u, w_ref[:, ATT_COLS + c0:ATT_COLS + c0 + cw])


def _inproj(h, nw, w, tm=512):
    n = h.shape[0]
    return pl.pallas_call(
        _inproj_body,
        grid=(n // tm,),
        in_specs=[
            pl.BlockSpec((tm, D_MODEL), lambda i: (i, 0)),
            pl.BlockSpec((1, D_MODEL), lambda i: (0, 0)),
            pl.BlockSpec((D_MODEL, D_IN_PAD), lambda i: (0, 0)),
        ],
        out_specs=[
            pl.BlockSpec((tm, ATT_COLS), lambda i: (i, 0)),
            pl.BlockSpec((tm, REST_COLS), lambda i: (i, 0)),
        ],
        out_shape=[
            jax.ShapeDtypeStruct((n, ATT_COLS), BF16),
            jax.ShapeDtypeStruct((n, REST_COLS), F32),
        ],
        compiler_params=_cparams(("arbitrary",)),
        name="inproj",
    )(h, nw, w)


def _attn_body(lamp_ref, subw_ref, q_ref, k_ref, v_ref, o_ref,
               vt_scr, qt_scr, m_scr, l_scr, acc_scr, *, tq, lambda_init):
    i = pl.program_id(2)
    nblk = vt_scr.shape[0]

    @pl.when(i == 0)
    def _():
        for c in range(nblk):
            vt_scr[c] = v_ref[c * tq:(c + 1) * tq, :].astype(F32).T.astype(BF16)

    qt = q_ref[...].astype(F32).T
    first = lax.broadcasted_iota(jnp.int32, qt.shape, 0) < ATT_QK_DIM
    qt_scr[:, :tq] = jnp.where(first, qt, 0.0).astype(BF16)
    qt_scr[:, tq:] = jnp.where(first, 0.0, qt).astype(BF16)
    m_scr[...] = jnp.full(m_scr.shape, MASK_NEG, F32)
    l_scr[...] = jnp.zeros(l_scr.shape, F32)
    acc_scr[...] = jnp.zeros(acc_scr.shape, F32)

    def step(j, masked):
        kj = k_ref[pl.ds(pl.multiple_of(j * tq, tq), tq), :]
        s = _dot(kj, qt_scr[...])
        if masked:
            kpos = lax.broadcasted_iota(jnp.int32, s.shape, 0)
            qpos = lax.broadcasted_iota(jnp.int32, s.shape, 1)
            qpos = jnp.where(qpos >= tq, qpos - tq, qpos)
            s = jnp.where(kpos <= qpos, s, MASK_NEG)
        m_old = m_scr[...]
        m_new = jnp.maximum(m_old, jnp.max(s, axis=0, keepdims=True))
        alpha = jnp.exp(m_old - m_new)
        p = jnp.exp(s - m_new)
        l_scr[...] = alpha * l_scr[...] + jnp.sum(p, axis=0, keepdims=True)
        acc_scr[...] = alpha * acc_scr[...] + _dot(vt_scr[j], p.astype(BF16))
        m_scr[...] = m_new

    def body(j, carry):
        step(j, False)
        return carry

    lax.fori_loop(0, i, body, 0)
    step(i, True)

    lp = lamp_ref[...]
    lam = (jnp.exp(jnp.sum(lp[0:1] * lp[1:2], axis=-1, keepdims=True))
           - jnp.exp(jnp.sum(lp[2:3] * lp[3:4], axis=-1, keepdims=True)) + lambda_init)
    l = l_scr[...]
    acc = acc_scr[...]
    ot = acc[:, :tq] / l[:, :tq] - lam * (acc[:, tq:] / l[:, tq:])
    o = _rms(ot.T, subw_ref[...]) * (1.0 - lambda_init)
    o_ref[...] = o.astype(BF16)


def _attention(att, lamp, subw, batch, seq, lambda_init, tq=256):
    n = att.shape[0]
    nq = seq // tq
    body = functools.partial(_attn_body, tq=tq, lambda_init=lambda_init)
    return pl.pallas_call(
        body,
        grid=(batch, ATT_HEADS, nq),
        in_specs=[
            pl.BlockSpec((4, ATT_QK_DIM), lambda b, h, i: (0, 0)),
            pl.BlockSpec((1, ATT_V_DIM), lambda b, h, i: (0, 0)),
            pl.BlockSpec((tq, LANES), lambda b, h, i: (b * nq + i, h)),
            pl.BlockSpec((seq, LANES), lambda b, h, i: (b, ATT_HEADS + h)),
            pl.BlockSpec((seq, LANES), lambda b, h, i: (b, 2 * ATT_HEADS + h)),
        ],
        out_specs=pl.BlockSpec((tq, LANES), lambda b, h, i: (b * nq + i, h)),
        out_shape=jax.ShapeDtypeStruct((n, ATT_WIDTH), BF16),
        scratch_shapes=[
            pltpu.VMEM((nq, LANES, tq), BF16),
            pltpu.VMEM((LANES, 2 * tq), BF16),
            pltpu.VMEM((1, 2 * tq), F32),
            pltpu.VMEM((1, 2 * tq), F32),
            pltpu.VMEM((LANES, 2 * tq), F32),
        ],
        compiler_params=_cparams(("arbitrary", "arbitrary", "arbitrary")),
        name="diff_attention",
    )(lamp, subw, att, att, att)


def _hgrn_body(x_ref, lbl_ref, nw_ref, o_ref, st_scr, b_scr, q_scr, k_scr, v_scr,
               qd_scr, kd_scr, dc_scr, oi_scr, *, layer, rows):
    @pl.when(pl.program_id(1) == 0)
    def _():
        st_scr[...] = jnp.zeros(st_scr.shape, F32)

    lg = lbl_ref[...]
    e = jnp.exp(lg - jnp.max(lg, axis=0, keepdims=True))
    sm = e / jnp.sum(e, axis=0, keepdims=True)
    lb = jnp.zeros((1, REC_WIDTH), F32)
    for i in range(1, layer + 1):
        lb = lb + sm[i:i + 1]

    x = x_ref[...]
    q = x[:, 0:REC_WIDTH]
    fp = x[:, REC_WIDTH:2 * REC_WIDTH]
    iv = x[:, 2 * REC_WIDTH:3 * REC_WIDTH]
    gate = x[:, 3 * REC_WIDTH:4 * REC_WIDTH]
    qf = _silu(q)
    f = lb + (1.0 - lb) * _sigmoid(fp)
    logf = jnp.log(jnp.maximum(f, F_FLOOR))
    kf = (1.0 - lb) * _sigmoid(-fp)

    incl, total = _chunk_cumsum_mats(rows, 4)
    b = _dot(incl, logf, HIGHEST)
    ball = _dot(total, logf, HIGHEST)
    b_scr[...] = b
    q_scr[...] = qf
    k_scr[...] = kf
    v_scr[...] = iv
    qd_scr[...] = qf * jnp.exp(b)
    kd_scr[...] = kf * jnp.exp(ball - b)
    dc_scr[...] = jnp.exp(ball)

    same = _same_head((REC_WIDTH, REC_WIDTH))
    ind = jnp.where(same, 1.0, 0.0).astype(BF16)
    srow = lax.broadcasted_iota(jnp.int32, (HG_CHUNK, REC_WIDTH), 0)

    def chunk(c, carry):
        base = pl.multiple_of(c * HG_CHUNK, HG_CHUNK)
        sl = pl.ds(base, HG_CHUNK)
        st = st_scr[...]
        o_inter = lax.dot_general(qd_scr[sl, :].astype(BF16), st.astype(BF16),
                                  (((1,), (1,)), ((), ())), preferred_element_type=F32)
        kvt = lax.dot_general(v_scr[sl, :].astype(BF16), kd_scr[sl, :].astype(BF16),
                              (((0,), (0,)), ((), ())), preferred_element_type=F32)
        st_scr[...] = st * dc_scr[pl.ds(base, 1), :] + jnp.where(same, kvt, 0.0)
        oi_scr[sl, :] = o_inter
        b_c = b_scr[sl, :]
        k_c = k_scr[sl, :]
        v_c = v_scr[sl, :]
        for t in range(HG_CHUNK):
            row = pl.ds(base + t, 1)
            w = jnp.exp(jnp.where(srow <= t, b_scr[row, :] - b_c, MASK_NEG)) * k_c * q_scr[row, :]
            a = _dot(w.astype(BF16), ind)
            oi_scr[row, :] = oi_scr[row, :] + jnp.sum(a * v_c, axis=0, keepdims=True)
        return carry

    lax.fori_loop(0, rows // HG_CHUNK, chunk, 0)

    o = oi_scr[...]
    ms = _dot((o * o).astype(BF16), ind) * (1.0 / HEAD_DIM)
    y = o * lax.rsqrt(ms + EPS) * nw_ref[...] * _silu(gate)
    o_ref[...] = y.astype(BF16)


def _hgrn(rest, lb_logits, nw, batch, seq, layer, rows=256):
    n = rest.shape[0]
    nb = seq // rows
    body = functools.partial(_hgrn_body, layer=layer, rows=rows)
    rec = pltpu.VMEM((rows, REC_WIDTH), F32)
    return pl.pallas_call(
        body,
        grid=(batch, nb),
        in_specs=[
            pl.BlockSpec((rows, HG_COLS), lambda b, r: (b * nb + r, 0)),
            pl.BlockSpec((DEPTH, REC_WIDTH), lambda b, r: (0, 0)),
            pl.BlockSpec((1, REC_WIDTH), lambda b, r: (0, 0)),
        ],
        out_specs=pl.BlockSpec((rows, REC_WIDTH), lambda b, r: (b * nb + r, 0)),
        out_shape=jax.ShapeDtypeStruct((n, REC_WIDTH), BF16),
        scratch_shapes=[pltpu.VMEM((REC_WIDTH, REC_WIDTH), F32)] + [rec] * 8,
        compiler_params=_cparams(("arbitrary", "arbitrary")),
        name="hgrn2",
    )(rest, lb_logits, nw)


def _dn_body(x_ref, ab_ref, cw_ref, par_ref, nw_ref, o_ref, carry_scr, xs_scr, s_scr, *, rows):
    @pl.when(pl.program_id(1) == 0)
    def _():
        carry_scr[...] = jnp.zeros(carry_scr.shape, F32)
        s_scr[...] = jnp.zeros(s_scr.shape, F32)

    x = x_ref[...]
    z = x[:, DN_CONV_CH:]
    xs_scr[0:8, :] = carry_scr[...]
    xs_scr[8:, :] = x[:, :DN_CONV_CH]
    carry_scr[...] = x[rows - 8:, :DN_CONV_CH]
    cw = cw_ref[...]
    y = cw[0:1] * xs_scr[5:5 + rows, :]
    for j in range(1, DN_CONV):
        y = y + cw[j:j + 1] * xs_scr[5 + j:5 + j + rows, :]
    y = _silu(y)
    qc = y[:, 0:REC_WIDTH]
    kc = y[:, REC_WIDTH:2 * REC_WIDTH]
    vc = y[:, 2 * REC_WIDTH:]

    same = _same_head((REC_WIDTH, REC_WIDTH))
    ind = jnp.where(same, 1.0, 0.0).astype(BF16)
    qn = qc * lax.rsqrt(_dot((qc * qc).astype(BF16), ind) + EPS) * (HEAD_DIM ** -0.5)
    kn = kc * lax.rsqrt(_dot((kc * kc).astype(BF16), ind) + EPS)

    er = lax.broadcasted_iota(jnp.int32, (LANES, REC_WIDTH), 0)
    ec = _head_of(lax.broadcasted_iota(jnp.int32, (LANES, REC_WIDTH), 1))
    sel_a = jnp.where(er == ec, 1.0, 0.0).astype(F32)
    sel_b = jnp.where(er == ec + HEADS, 1.0, 0.0).astype(F32)
    ab = ab_ref[...]
    a_in = _dot(ab, sel_a, HIGHEST)
    b_in = _dot(ab, sel_b, HIGHEST)
    par = _dot(par_ref[...], sel_a, HIGHEST)
    sp_in = a_in + par[1:2]
    softplus = jnp.maximum(sp_in, 0.0) + jnp.log1p(jnp.exp(-jnp.abs(sp_in)))
    g = -jnp.exp(par[0:1]) * softplus
    beta = _sigmoid(b_in)

    incl, total = _chunk_cumsum_mats(rows, 6)
    gc = _dot(incl, g, HIGHEST)
    gl = _dot(total, g, HIGHEST)
    eg = jnp.exp(gc)
    kb = kn * beta
    vb = vc * beta
    kbe = kb * eg
    q_dec = qn * eg
    g_last = jnp.exp(gl)

    crow = lax.broadcasted_iota(jnp.int32, (DN_CHUNK, REC_WIDTH), 0)
    slane = lax.broadcasted_iota(jnp.int32, (DN_CHUNK, REC_WIDTH), 1) & (HEAD_DIM - 1)
    diag = crow == slane
    eye = jnp.where(diag, 1.0, 0.0).astype(F32)

    outs = []
    s_bd = s_scr[...]
    for n in range(rows // DN_CHUNK):
        sl = slice(n * DN_CHUNK, (n + 1) * DN_CHUNK)
        gc_c = gc[sl]
        g_row = jnp.sum(jnp.where(diag, gc_c, 0.0), axis=0, keepdims=True)
        decay = jnp.exp(jnp.where(slane <= crow, gc_c - g_row, MASK_NEG))
        kt_bd = jnp.where(same, jnp.concatenate([kn[sl]] * HEADS, axis=0).T, 0.0)
        sc = _dot(jnp.concatenate([kb[sl], qn[sl]], axis=0).astype(BF16), kt_bd.astype(BF16))
        a_kk = jnp.where(slane < crow, sc[:DN_CHUNK] * decay, 0.0)
        a_qk = sc[DN_CHUNK:] * decay
        xinv = eye - a_kk
        pw = _dot(a_kk.astype(BF16), _block_diag(a_kk, same))
        for _ in range(4):
            r = _dot(jnp.concatenate([xinv, pw], axis=0).astype(BF16), _block_diag(pw, same))
            xinv = xinv + r[:DN_CHUNK]
            pw = r[DN_CHUNK:]
        xinv = xinv + _dot(xinv.astype(BF16), _block_diag(pw, same))
        rhs = jnp.concatenate([_block_diag(vb[sl], same), _block_diag(kbe[sl], same)], axis=1)
        uw = _dot(xinv.astype(BF16), rhs)
        u = uw[:, :REC_WIDTH]
        w = uw[:, REC_WIDTH:]
        r1 = _dot(jnp.concatenate([w, q_dec[sl]], axis=0).astype(BF16), s_bd.astype(BF16))
        v_new = u - r1[:DN_CHUNK]
        gl_row = gl[n * DN_CHUNK:n * DN_CHUNK + 1]
        kdt_bd = kt_bd * jnp.exp(gl_row - g_row)
        r2 = _dot(jnp.concatenate([a_qk, kdt_bd], axis=0).astype(BF16), _block_diag(v_new, same))
        outs.append(r1[DN_CHUNK:] + r2[:DN_CHUNK])
        s_bd = s_bd * g_last[n * DN_CHUNK:n * DN_CHUNK + 1] + r2[DN_CHUNK:]
    s_scr[...] = s_bd

    o = jnp.concatenate(outs, axis=0)
    ms = _dot((o * o).astype(BF16), ind) * (1.0 / HEAD_DIM)
    o_ref[...] = (o * lax.rsqrt(ms + EPS) * nw_ref[...] * _silu(z)).astype(BF16)


def _deltanet(rest, cw, par, nw, batch, seq, rows=256):
    n = rest.shape[0]
    nb = seq // rows
    body = functools.partial(_dn_body, rows=rows)
    return pl.pallas_call(
        body,
        grid=(batch, nb),
        in_specs=[
            pl.BlockSpec((rows, DN_COLS), lambda b, r: (b * nb + r, HG_COLS // DN_COLS)),
            pl.BlockSpec((rows, AB_COLS), lambda b, r: (b * nb + r, (HG_COLS + DN_COLS) // AB_COLS)),
            pl.BlockSpec((DN_CONV, DN_CONV_CH), lambda b, r: (0, 0)),
            pl.BlockSpec((8, LANES), lambda b, r: (0, 0)),
            pl.BlockSpec((1, REC_WIDTH), lambda b, r: (0, 0)),
        ],
        out_specs=pl.BlockSpec((rows, REC_WIDTH), lambda b, r: (b * nb + r, 0)),
        out_shape=jax.ShapeDtypeStruct((n, REC_WIDTH), BF16),
        scratch_shapes=[
            pltpu.VMEM((8, DN_CONV_CH), F32),
            pltpu.VMEM((rows + 8, DN_CONV_CH), F32),
            pltpu.VMEM((REC_WIDTH, REC_WIDTH), F32),
        ],
        compiler_params=_cparams(("arbitrary", "arbitrary")),
        name="gated_deltanet",
    )(rest, rest, cw, par, nw)


def _outproj_body(att_ref, hg_ref, dn_ref, h_ref, w_ref, o_ref):
    acc = h_ref[...] + _dot(att_ref[...], w_ref[0:ATT_WIDTH, :])
    acc = acc + _dot(hg_ref[...], w_ref[ATT_WIDTH:ATT_WIDTH + REC_WIDTH, :])
    o_ref[...] = acc + _dot(dn_ref[...], w_ref[ATT_WIDTH + REC_WIDTH:, :])


def _outproj(att_o, hg_o, dn_o, h, w, tm=512):
    n = h.shape[0]
    return pl.pallas_call(
        _outproj_body,
        grid=(n // tm,),
        in_specs=[
            pl.BlockSpec((tm, ATT_WIDTH), lambda i: (i, 0)),
            pl.BlockSpec((tm, REC_WIDTH), lambda i: (i, 0)),
            pl.BlockSpec((tm, REC_WIDTH), lambda i: (i, 0)),
            pl.BlockSpec((tm, D_MODEL), lambda i: (i, 0)),
            pl.BlockSpec((D_MODEL, D_MODEL), lambda i: (0, 0)),
        ],
        out_specs=pl.BlockSpec((tm, D_MODEL), lambda i: (i, 0)),
        out_shape=jax.ShapeDtypeStruct((n, D_MODEL), F32),
        compiler_params=_cparams(("arbitrary",)),
        name="outproj",
    )(att_o, hg_o, dn_o, h, w)


def _ffn_body(x_ref, halo_ref, nw_ref, wup_ref, cw_ref, cb_ref, wdn_ref, fw_ref, o_ref, u_scr,
              *, tm, tf, blocks_per_seq, final):
    i = pl.program_id(0)
    nw = nw_ref[...]
    x = x_ref[...]
    u_scr[HALO:, :] = _rms(x, nw).astype(BF16)
    halo = _rms(halo_ref[...], nw)
    u_scr[:HALO, :] = jnp.where(i % blocks_per_seq == 0, 0.0, halo).astype(BF16)
    u = u_scr[...]
    o_ref[...] = x

    def conv(zz, w, b):
        lo = HALO - (FFN_CONV - 1)
        out = b
        for j in range(FFN_CONV):
            out = out + w[j:j + 1] * zz[lo + j:lo + j + tm]
        return out

    for j in range(D_FF // tf):
        gs = slice(j * tf, (j + 1) * tf)
        vs = slice(D_FF + j * tf, D_FF + (j + 1) * tf)
        gate = conv(_dot(u, wup_ref[:, gs]), cw_ref[:, gs], cb_ref[:, gs])
        val = conv(_dot(u, wup_ref[:, vs]), cw_ref[:, vs], cb_ref[:, vs])
        act = (_silu(gate) * val).astype(BF16)
        o_ref[...] += _dot(act, wdn_ref[gs, :])

    if final:
        o_ref[...] = _rms(o_ref[...], fw_ref[...])


def _ffn(h, nw, wup, cw, cb, wdn, fw, seq, final, tm=512, tf=256):
    n = h.shape[0]
    body = functools.partial(_ffn_body, tm=tm, tf=tf, blocks_per_seq=seq // tm, final=final)
    hb = tm // HALO
    return pl.pallas_call(
        body,
        grid=(n // tm,),
        in_specs=[
            pl.BlockSpec((tm, D_MODEL), lambda i: (i, 0)),
            pl.BlockSpec((HALO, D_MODEL), lambda i: (jnp.maximum(i * hb - 1, 0), 0)),
            pl.BlockSpec((1, D_MODEL), lambda i: (0, 0)),
            pl.BlockSpec((D_MODEL, 2 * D_FF), lambda i: (0, 0)),
            pl.BlockSpec((FFN_CONV, 2 * D_FF), lambda i: (0, 0)),
            pl.BlockSpec((1, 2 * D_FF), lambda i: (0, 0)),
            pl.BlockSpec((D_FF, D_MODEL), lambda i: (0, 0)),
            pl.BlockSpec((1, D_MODEL), lambda i: (0, 0)),
        ],
        out_specs=pl.BlockSpec((tm, D_MODEL), lambda i: (i, 0)),
        out_shape=jax.ShapeDtypeStruct((n, D_MODEL), F32),
        scratch_shapes=[pltpu.VMEM((HALO + tm, D_MODEL), BF16)],
        compiler_params=_cparams(("arbitrary",)),
        name="conv_ffn",
    )(h, h, nw, wup, cw, cb, wdn, fw)


def _prep_w_in(w_in):
    scale = jnp.concatenate([jnp.full((ATT_WIDTH,), ATT_QK_DIM ** -0.5, F32),
                             jnp.ones((D_IN - ATT_WIDTH,), F32)])
    w = w_in * scale
    w = jnp.pad(w, ((0, 0), (0, 0), (0, D_IN_PAD - D_IN)))
    return w.astype(BF16)


def kernel(x, attn_norm_w, w_in, diff_lambda, diff_subln_w, hgrn_lb_logits, hgrn_norm_w, dn_conv_w,
           dn_A_log, dn_dt_bias, dn_norm_w, w_out, ffn_norm_w, ffn_w_up, ffn_conv_w, ffn_conv_b,
           ffn_w_down, final_norm_w):
    batch, seq, _ = x.shape
    h = x.reshape(batch * seq, D_MODEL)
    w_in_p = _prep_w_in(w_in)
    w_out_b = w_out.astype(BF16)
    w_up_b = ffn_w_up.astype(BF16)
    w_dn_b = ffn_w_down.astype(BF16)
    dn_par = jnp.zeros((DEPTH, 8, LANES), F32)
    dn_par = dn_par.at[:, 0, :HEADS].set(dn_A_log).at[:, 1, :HEADS].set(dn_dt_bias)
    hg_nw = jnp.tile(hgrn_norm_w, (1, HEADS))
    dn_nw = jnp.tile(dn_norm_w, (1, HEADS))
    for l in range(DEPTH):
        lambda_init = 0.8 - 0.6 * math.exp(-0.3 * l)
        att, rest = _inproj(h, attn_norm_w[l][None], w_in_p[l])
        att_o = _attention(att, diff_lambda[l], diff_subln_w[l][None], batch, seq, lambda_init)
        hg_o = _hgrn(rest, hgrn_lb_logits, hg_nw[l][None], batch, seq, l)
        dn_o = _deltanet(rest, dn_conv_w[l], dn_par[l], dn_nw[l][None], batch, seq)
        h = _outproj(att_o, hg_o, dn_o, h, w_out_b[l])
        h = _ffn(h, ffn_norm_w[l][None], w_up_b[l], ffn_conv_w[l], ffn_conv_b[l][None], w_dn_b[l],
                 final_norm_w[None], seq, final=(l == DEPTH - 1))
    return h.reshape(batch, seq, D_MODEL)
```

```python
import functools
import math

import jax
import jax.numpy as jnp
from jax import lax
from jax.experimental import pallas as pl
from jax.experimental.pallas import tpu as pltpu

F32 = jnp.float32
BF16 = jnp.bfloat16
HIGHEST = lax.Precision.HIGHEST

D_MODEL = 1024
DEPTH = 2
ATT_QK_DIM = 64
ATT_V_DIM = 128
ATT_HEADS = 4
ATT_WIDTH = 512
HEADS = 4
HEAD_DIM = 64
REC_WIDTH = HEADS * HEAD_DIM
DN_CONV = 4
DN_CONV_CH = 3 * REC_WIDTH
FFN_CONV = 3
D_FF = 2816
HG_CHUNK = 16
DN_CHUNK = 64
EPS = 1e-6
MASK_NEG = -1e30
F_FLOOR = 1e-30
LOG2E = math.log2(math.e)

ATT_COLS = 3 * ATT_WIDTH
HG_COLS = 4 * REC_WIDTH
DN_COLS = 4 * REC_WIDTH
AB_COLS = 128
REST_COLS = HG_COLS + DN_COLS + AB_COLS
D_IN = ATT_COLS + HG_COLS + DN_COLS + 2 * HEADS
D_IN_PAD = ATT_COLS + REST_COLS

VMEM_LIMIT = 56 * 1024 * 1024
LANES = 128
HALO = 16


def _cparams(sem):
    return pltpu.CompilerParams(dimension_semantics=sem, vmem_limit_bytes=VMEM_LIMIT)


def _rms(x, w):
    return x * lax.rsqrt(jnp.mean(x * x, axis=-1, keepdims=True) + EPS) * w


def _sigmoid(x):
    return 1.0 / (1.0 + jnp.exp(-x))


def _silu(x):
    return x * _sigmoid(x)


def _dot(a, b, precision=None):
    return jnp.dot(a, b, preferred_element_type=F32, precision=precision)


def _head_of(idx):
    return lax.shift_right_logical(idx, 6)


def _same_head(shape):
    r = lax.broadcasted_iota(jnp.int32, shape, 0)
    c = lax.broadcasted_iota(jnp.int32, shape, 1)
    return _head_of(r) == _head_of(c)


def _block_diag(x, mask):
    return jnp.where(mask, jnp.concatenate([x] * HEADS, axis=0), 0.0).astype(BF16)


def _chunk_cumsum_mats(rows, chunk_shift):
    r = lax.broadcasted_iota(jnp.int32, (rows, rows), 0)
    c = lax.broadcasted_iota(jnp.int32, (rows, rows), 1)
    same = lax.shift_right_logical(r, chunk_shift) == lax.shift_right_logical(c, chunk_shift)
    incl = jnp.where(same & (c <= r), 1.0, 0.0).astype(F32)
    total = jnp.where(same, 1.0, 0.0).astype(F32)
    return incl, total


def _inproj_body(x_ref, nw_ref, w_ref, att_ref, rest_ref):
    u = _rms(x_ref[...], nw_ref[...]).astype(BF16)
    step = 512
    for c0 in range(0, ATT_COLS, step):
        att_ref[:, c0:c0 + step] = _dot(u, w_ref[:, c0:c0 + step]).astype(BF16)
    for c0 in range(0, REST_COLS, step):
        cw = min(step, REST_COLS - c0)
        rest_ref[:, c0:c0 + cw] = _dot(u, w_ref[:, ATT_COLS + c0:ATT_COLS + c0 + cw])


def _inproj(h, nw, w, tm=512):
    n = h.shape[0]
    return pl.pallas_call(
        _inproj_body,
        grid=(n // tm,),
        in_specs=[
            pl.BlockSpec((tm, D_MODEL), lambda i: (i, 0)),
            pl.BlockSpec((1, D_MODEL), lambda i: (0, 0)),
            pl.BlockSpec((D_MODEL, D_IN_PAD), lambda i: (0, 0)),
        ],
        out_specs=[
            pl.BlockSpec((tm, ATT_COLS), lambda i: (i, 0)),
            pl.BlockSpec((tm, REST_COLS), lambda i: (i, 0)),
        ],
        out_shape=[
            jax.ShapeDtypeStruct((n, ATT_COLS), BF16),
            jax.ShapeDtypeStruct((n, REST_COLS), F32),
        ],
        compiler_params=_cparams(("arbitrary",)),
        name="inproj",
    )(h, nw, w)


def _attn_body(lamp_ref, subw_ref, q_ref, k_ref, v_ref, o_ref,
               vt_scr, qt_scr, sa_scr, sb_scr, pa_scr, pb_scr, m_scr, l_scr, al_scr, acc_scr,
               *, tq, lambda_init):
    i = pl.program_id(2)
    nblk = vt_scr.shape[0]

    @pl.when(i == 0)
    def _():
        for c in range(nblk):
            vt_scr[c] = v_ref[c * tq:(c + 1) * tq, :].astype(F32).T.astype(BF16)

    qt = q_ref[...].astype(F32).T * LOG2E
    first = lax.broadcasted_iota(jnp.int32, qt.shape, 0) < ATT_QK_DIM
    qt_scr[:, :tq] = jnp.where(first, qt, 0.0).astype(BF16)
    qt_scr[:, tq:] = jnp.where(first, 0.0, qt).astype(BF16)
    m_scr[...] = jnp.full(m_scr.shape, MASK_NEG, F32)
    l_scr[...] = jnp.zeros(l_scr.shape, F32)
    al_scr[...] = jnp.ones(al_scr.shape, F32)
    acc_scr[...] = jnp.zeros(acc_scr.shape, F32)
    pb_scr[...] = jnp.zeros(pb_scr.shape, BF16)

    def scores(j):
        kj = k_ref[pl.ds(pl.multiple_of(j * tq, tq), tq), :]
        return _dot(kj, qt_scr[...])

    def phase(j, s_cur, p_cur, s_nxt, p_prv, masked, prefetch=True):
        pv = _dot(vt_scr[jnp.maximum(j - 1, 0)], p_prv[...])
        if prefetch:
            s_nxt[...] = scores(j + 1)
        s = s_cur[...]
        if masked:
            kpos = lax.broadcasted_iota(jnp.int32, s.shape, 0)
            qpos = lax.broadcasted_iota(jnp.int32, s.shape, 1)
            qpos = jnp.where(qpos >= tq, qpos - tq, qpos)
            s = jnp.where(kpos <= qpos, s, MASK_NEG)
        m_old = m_scr[...]
        m_new = jnp.maximum(m_old, jnp.max(s, axis=0, keepdims=True))
        alpha = jnp.exp2(m_old - m_new)
        p = jnp.exp2(s - m_new)
        l_scr[...] = alpha * l_scr[...] + jnp.sum(p, axis=0, keepdims=True)
        m_scr[...] = m_new
        p_cur[...] = p.astype(BF16)
        acc_scr[...] = al_scr[...] * acc_scr[...] + pv
        al_scr[...] = alpha

    def finish(j, p_cur):
        acc_scr[...] = al_scr[...] * acc_scr[...] + _dot(vt_scr[j], p_cur[...])

    sa_scr[...] = scores(0)

    def pair(jj, carry):
        phase(2 * jj, sa_scr, pa_scr, sb_scr, pb_scr, False)
        phase(2 * jj + 1, sb_scr, pb_scr, sa_scr, pa_scr, False)
        return carry

    lax.fori_loop(0, lax.shift_right_logical(i, 1), pair, 0)

    @pl.when((i & 1) == 0)
    def _():
        phase(i, sa_scr, pa_scr, sb_scr, pb_scr, True, prefetch=False)
        finish(i, pa_scr)

    @pl.when((i & 1) == 1)
    def _():
        phase(i - 1, sa_scr, pa_scr, sb_scr, pb_scr, False)
        phase(i, sb_scr, pb_scr, sa_scr, pa_scr, True, prefetch=False)
        finish(i, pb_scr)

    acc = acc_scr[...]
    lp = lamp_ref[...]
    lam = (jnp.exp(jnp.sum(lp[0:1] * lp[1:2], axis=-1, keepdims=True))
           - jnp.exp(jnp.sum(lp[2:3] * lp[3:4], axis=-1, keepdims=True)) + lambda_init)
    l = l_scr[...]
    ot = acc[:, :tq] / l[:, :tq] - lam * (acc[:, tq:] / l[:, tq:])
    o = _rms(ot.T, subw_ref[...]) * (1.0 - lambda_init)
    o_ref[...] = o.astype(BF16)


def _attention(att, lamp, subw, batch, seq, lambda_init, tq=512):
    n = att.shape[0]
    nq = seq // tq
    body = functools.partial(_attn_body, tq=tq, lambda_init=lambda_init)
    return pl.pallas_call(
        body,
        grid=(batch, ATT_HEADS, nq),
        in_specs=[
            pl.BlockSpec((4, ATT_QK_DIM), lambda b, h, i: (0, 0)),
            pl.BlockSpec((1, ATT_V_DIM), lambda b, h, i: (0, 0)),
            pl.BlockSpec((tq, LANES), lambda b, h, i: (b * nq + i, h)),
            pl.BlockSpec((seq, LANES), lambda b, h, i: (b, ATT_HEADS + h)),
            pl.BlockSpec((seq, LANES), lambda b, h, i: (b, 2 * ATT_HEADS + h)),
        ],
        out_specs=pl.BlockSpec((tq, LANES), lambda b, h, i: (b * nq + i, h)),
        out_shape=jax.ShapeDtypeStruct((n, ATT_WIDTH), BF16),
        scratch_shapes=[
            pltpu.VMEM((nq, LANES, tq), BF16),
            pltpu.VMEM((LANES, 2 * tq), BF16),
            pltpu.VMEM((tq, 2 * tq), F32),
            pltpu.VMEM((tq, 2 * tq), F32),
            pltpu.VMEM((tq, 2 * tq), BF16),
            pltpu.VMEM((tq, 2 * tq), BF16),
            pltpu.VMEM((1, 2 * tq), F32),
            pltpu.VMEM((1, 2 * tq), F32),
            pltpu.VMEM((1, 2 * tq), F32),
            pltpu.VMEM((LANES, 2 * tq), F32),
        ],
        compiler_params=_cparams(("arbitrary", "arbitrary", "arbitrary")),
        name="diff_attention",
    )(lamp, subw, att, att, att)


def _hgrn_body(x_ref, lbl_ref, nw_ref, o_ref, st_scr, b_scr, q_scr, k_scr, v_scr,
               qd_scr, kd_scr, dc_scr, oi_scr, *, layer, rows):
    @pl.when(pl.program_id(1) == 0)
    def _():
        st_scr[...] = jnp.zeros(st_scr.shape, F32)

    lg = lbl_ref[...]
    e = jnp.exp(lg - jnp.max(lg, axis=0, keepdims=True))
    sm = e / jnp.sum(e, axis=0, keepdims=True)
    lb = jnp.zeros((1, REC_WIDTH), F32)
    for i in range(1, layer + 1):
        lb = lb + sm[i:i + 1]

    x = x_ref[...]
    q = x[:, 0:REC_WIDTH]
    fp = x[:, REC_WIDTH:2 * REC_WIDTH]
    iv = x[:, 2 * REC_WIDTH:3 * REC_WIDTH]
    gate = x[:, 3 * REC_WIDTH:4 * REC_WIDTH]
    qf = _silu(q)
    f = lb + (1.0 - lb) * _sigmoid(fp)
    logf = jnp.log(jnp.maximum(f, F_FLOOR))
    kf = (1.0 - lb) * _sigmoid(-fp)

    incl, total = _chunk_cumsum_mats(rows, 4)
    b = _dot(incl, logf, HIGHEST)
    ball = _dot(total, logf, HIGHEST)
    b_scr[...] = b
    q_scr[...] = qf
    k_scr[...] = kf
    v_scr[...] = iv
    qd_scr[...] = qf * jnp.exp(b)
    kd_scr[...] = kf * jnp.exp(ball - b)
    dc_scr[...] = jnp.exp(ball)

    same = _same_head((REC_WIDTH, REC_WIDTH))
    ind = jnp.where(same, 1.0, 0.0).astype(BF16)
    trow = lax.broadcasted_iota(jnp.int32, (HG_CHUNK, REC_WIDTH), 0)

    def chunk(c, carry):
        base = pl.multiple_of(c * HG_CHUNK, HG_CHUNK)
        sl = pl.ds(base, HG_CHUNK)
        st = st_scr[...]
        o_inter = lax.dot_general(qd_scr[sl, :].astype(BF16), st.astype(BF16),
                                  (((1,), (1,)), ((), ())), preferred_element_type=F32)
        kvt = lax.dot_general(v_scr[sl, :].astype(BF16), kd_scr[sl, :].astype(BF16),
                              (((0,), (0,)), ((), ())), preferred_element_type=F32)
        st_scr[...] = st * dc_scr[pl.ds(base, 1), :] + jnp.where(same, kvt, 0.0)
        b_c = b_scr[sl, :]
        q_c = q_scr[sl, :]
        slabs = []
        for s in range(HG_CHUNK):
            row = pl.ds(base + s, 1)
            rel = jnp.where(trow >= s, b_c - b_scr[row, :], MASK_NEG)
            slabs.append((jnp.exp(rel) * q_c * k_scr[row, :]).astype(BF16))
        a = _dot(jnp.concatenate(slabs, axis=0), ind)
        o_intra = a[0:HG_CHUNK] * v_scr[pl.ds(base, 1), :]
        for s in range(1, HG_CHUNK):
            o_intra = o_intra + a[s * HG_CHUNK:(s + 1) * HG_CHUNK] * v_scr[pl.ds(base + s, 1), :]
        oi_scr[sl, :] = o_inter + o_intra
        return carry

    lax.fori_loop(0, rows // HG_CHUNK, chunk, 0, unroll=4)

    o = oi_scr[...]
    ms = _dot((o * o).astype(BF16), ind) * (1.0 / HEAD_DIM)
    y = o * lax.rsqrt(ms + EPS) * nw_ref[...] * _silu(gate)
    o_ref[...] = y.astype(BF16)


def _hgrn(rest, lb_logits, nw, batch, seq, layer, rows=256):
    n = rest.shape[0]
    nb = seq // rows
    body = functools.partial(_hgrn_body, layer=layer, rows=rows)
    rec = pltpu.VMEM((rows, REC_WIDTH), F32)
    return pl.pallas_call(
        body,
        grid=(batch, nb),
        in_specs=[
            pl.BlockSpec((rows, HG_COLS), lambda b, r: (b * nb + r, 0)),
            pl.BlockSpec((DEPTH, REC_WIDTH), lambda b, r: (0, 0)),
            pl.BlockSpec((1, REC_WIDTH), lambda b, r: (0, 0)),
        ],
        out_specs=pl.BlockSpec((rows, REC_WIDTH), lambda b, r: (b * nb + r, 0)),
        out_shape=jax.ShapeDtypeStruct((n, REC_WIDTH), BF16),
        scratch_shapes=[pltpu.VMEM((REC_WIDTH, REC_WIDTH), F32)] + [rec] * 8,
        compiler_params=_cparams(("arbitrary", "arbitrary")),
        name="hgrn2",
    )(rest, lb_logits, nw)


def _dn_body(x_ref, ab_ref, cw_ref, par_ref, nw_ref, o_ref, carry_scr, xs_scr, s_scr, *, rows):
    @pl.when(pl.program_id(1) == 0)
    def _():
        carry_scr[...] = jnp.zeros(carry_scr.shape, F32)
        s_scr[...] = jnp.zeros(s_scr.shape, F32)

    x = x_ref[...]
    z = x[:, DN_CONV_CH:]
    xs_scr[0:8, :] = carry_scr[...]
    xs_scr[8:, :] = x[:, :DN_CONV_CH]
    carry_scr[...] = x[rows - 8:, :DN_CONV_CH]
    cw = cw_ref[...]
    y = cw[0:1] * xs_scr[5:5 + rows, :]
    for j in range(1, DN_CONV):
        y = y + cw[j:j + 1] * xs_scr[5 + j:5 + j + rows, :]
    y = _silu(y)
    qc = y[:, 0:REC_WIDTH]
    kc = y[:, REC_WIDTH:2 * REC_WIDTH]
    vc = y[:, 2 * REC_WIDTH:]

    same = _same_head((REC_WIDTH, REC_WIDTH))
    ind = jnp.where(same, 1.0, 0.0).astype(BF16)
    qn = qc * lax.rsqrt(_dot((qc * qc).astype(BF16), ind) + EPS) * (HEAD_DIM ** -0.5)
    kn = kc * lax.rsqrt(_dot((kc * kc).astype(BF16), ind) + EPS)

    er = lax.broadcasted_iota(jnp.int32, (LANES, REC_WIDTH), 0)
    ec = _head_of(lax.broadcasted_iota(jnp.int32, (LANES, REC_WIDTH), 1))
    sel_a = jnp.where(er == ec, 1.0, 0.0).astype(F32)
    sel_b = jnp.where(er == ec + HEADS, 1.0, 0.0).astype(F32)
    ab = ab_ref[...]
    a_in = _dot(ab, sel_a, HIGHEST)
    b_in = _dot(ab, sel_b, HIGHEST)
    par = _dot(par_ref[...], sel_a, HIGHEST)
    sp_in = a_in + par[1:2]
    softplus = jnp.maximum(sp_in, 0.0) + jnp.log1p(jnp.exp(-jnp.abs(sp_in)))
    g = -jnp.exp(par[0:1]) * softplus
    beta = _sigmoid(b_in)

    incl, total = _chunk_cumsum_mats(rows, 6)
    gc = _dot(incl, g, HIGHEST)
    gl = _dot(total, g, HIGHEST)
    eg = jnp.exp(gc)
    kb = kn * beta
    vb = vc * beta
    kbe = kb * eg
    q_dec = qn * eg
    g_last = jnp.exp(gl)

    crow = lax.broadcasted_iota(jnp.int32, (DN_CHUNK, REC_WIDTH), 0)
    slane = lax.broadcasted_iota(jnp.int32, (DN_CHUNK, REC_WIDTH), 1) & (HEAD_DIM - 1)
    diag = crow == slane
    eye = jnp.where(diag, 1.0, 0.0).astype(F32)

    outs = []
    s_bd = s_scr[...]
    for n in range(rows // DN_CHUNK):
        sl = slice(n * DN_CHUNK, (n + 1) * DN_CHUNK)
        gc_c = gc[sl]
        g_row = jnp.sum(jnp.where(diag, gc_c, 0.0), axis=0, keepdims=True)
        decay = jnp.exp(jnp.where(slane <= crow, gc_c - g_row, MASK_NEG))
        kt_bd = jnp.where(same, jnp.concatenate([kn[sl]] * HEADS, axis=0).T, 0.0)
        sc = _dot(jnp.concatenate([kb[sl], qn[sl]], axis=0).astype(BF16), kt_bd.astype(BF16))
        a_kk = jnp.where(slane < crow, sc[:DN_CHUNK] * decay, 0.0)
        a_qk = sc[DN_CHUNK:] * decay
        xinv = eye - a_kk
        pw = _dot(a_kk.astype(BF16), _block_diag(a_kk, same))
        for _ in range(4):
            r = _dot(jnp.concatenate([xinv, pw], axis=0).astype(BF16), _block_diag(pw, same))
            xinv = xinv + r[:DN_CHUNK]
            pw = r[DN_CHUNK:]
        xinv = xinv + _dot(xinv.astype(BF16), _block_diag(pw, same))
        rhs = jnp.concatenate([_block_diag(vb[sl], same), _block_diag(kbe[sl], same)], axis=1)
        uw = _dot(xinv.astype(BF16), rhs)
        u = uw[:, :REC_WIDTH]
        w = uw[:, REC_WIDTH:]
        r1 = _dot(jnp.concatenate([w, q_dec[sl]], axis=0).astype(BF16), s_bd.astype(BF16))
        v_new = u - r1[:DN_CHUNK]
        gl_row = gl[n * DN_CHUNK:n * DN_CHUNK + 1]
        kdt_bd = kt_bd * jnp.exp(gl_row - g_row)
        r2 = _dot(jnp.concatenate([a_qk, kdt_bd], axis=0).astype(BF16), _block_diag(v_new, same))
        outs.append(r1[DN_CHUNK:] + r2[:DN_CHUNK])
        s_bd = s_bd * g_last[n * DN_CHUNK:n * DN_CHUNK + 1] + r2[DN_CHUNK:]
    s_scr[...] = s_bd

    o = jnp.concatenate(outs, axis=0)
    ms = _dot((o * o).astype(BF16), ind) * (1.0 / HEAD_DIM)
    o_ref[...] = (o * lax.rsqrt(ms + EPS) * nw_ref[...] * _silu(z)).astype(BF16)


def _deltanet(rest, cw, par, nw, batch, seq, rows=256):
    n = rest.shape[0]
    nb = seq // rows
    body = functools.partial(_dn_body, rows=rows)
    return pl.pallas_call(
        body,
        grid=(batch, nb),
        in_specs=[
            pl.BlockSpec((rows, DN_COLS), lambda b, r: (b * nb + r, HG_COLS // DN_COLS)),
            pl.BlockSpec((rows, AB_COLS), lambda b, r: (b * nb + r, (HG_COLS + DN_COLS) // AB_COLS)),
            pl.BlockSpec((DN_CONV, DN_CONV_CH), lambda b, r: (0, 0)),
            pl.BlockSpec((8, LANES), lambda b, r: (0, 0)),
            pl.BlockSpec((1, REC_WIDTH), lambda b, r: (0, 0)),
        ],
        out_specs=pl.BlockSpec((rows, REC_WIDTH), lambda b, r: (b * nb + r, 0)),
        out_shape=jax.ShapeDtypeStruct((n, REC_WIDTH), BF16),
        scratch_shapes=[
            pltpu.VMEM((8, DN_CONV_CH), F32),
            pltpu.VMEM((rows + 8, DN_CONV_CH), F32),
            pltpu.VMEM((REC_WIDTH, REC_WIDTH), F32),
        ],
        compiler_params=_cparams(("arbitrary", "arbitrary")),
        name="gated_deltanet",
    )(rest, rest, cw, par, nw)


def _outproj_body(att_ref, hg_ref, dn_ref, h_ref, w_ref, o_ref):
    acc = h_ref[...] + _dot(att_ref[...], w_ref[0:ATT_WIDTH, :])
    acc = acc + _dot(hg_ref[...], w_ref[ATT_WIDTH:ATT_WIDTH + REC_WIDTH, :])
    o_ref[...] = acc + _dot(dn_ref[...], w_ref[ATT_WIDTH + REC_WIDTH:, :])


def _outproj(att_o, hg_o, dn_o, h, w, tm=512):
    n = h.shape[0]
    return pl.pallas_call(
        _outproj_body,
        grid=(n // tm,),
        in_specs=[
            pl.BlockSpec((tm, ATT_WIDTH), lambda i: (i, 0)),
            pl.BlockSpec((tm, REC_WIDTH), lambda i: (i, 0)),
            pl.BlockSpec((tm, REC_WIDTH), lambda i: (i, 0)),
            pl.BlockSpec((tm, D_MODEL), lambda i: (i, 0)),
            pl.BlockSpec((D_MODEL, D_MODEL), lambda i: (0, 0)),
        ],
        out_specs=pl.BlockSpec((tm, D_MODEL), lambda i: (i, 0)),
        out_shape=jax.ShapeDtypeStruct((n, D_MODEL), F32),
        compiler_params=_cparams(("arbitrary",)),
        name="outproj",
    )(att_o, hg_o, dn_o, h, w)


def _ffn_body(x_ref, halo_ref, nw_ref, wup_ref, cw_ref, cb_ref, wdn_ref, fw_ref, o_ref, u_scr,
              *, tm, tf, blocks_per_seq, final):
    i = pl.program_id(0)
    nw = nw_ref[...]
    x = x_ref[...]
    u_scr[HALO:, :] = _rms(x, nw).astype(BF16)
    halo = _rms(halo_ref[...], nw)
    u_scr[:HALO, :] = jnp.where(i % blocks_per_seq == 0, 0.0, halo).astype(BF16)
    u = u_scr[...]
    o_ref[...] = x

    def conv(zz, w, b):
        lo = HALO - (FFN_CONV - 1)
        out = b
        for j in range(FFN_CONV):
            out = out + w[j:j + 1] * zz[lo + j:lo + j + tm]
        return out

    for j in range(D_FF // tf):
        gs = slice(j * tf, (j + 1) * tf)
        vs = slice(D_FF + j * tf, D_FF + (j + 1) * tf)
        gate = conv(_dot(u, wup_ref[:, gs]), cw_ref[:, gs], cb_ref[:, gs])
        val = conv(_dot(u, wup_ref[:, vs]), cw_ref[:, vs], cb_ref[:, vs])
        act = (_silu(gate) * val).astype(BF16)
        o_ref[...] += _dot(act, wdn_ref[gs, :])

    if final:
        o_ref[...] = _rms(o_ref[...], fw_ref[...])


def _ffn(h, nw, wup, cw, cb, wdn, fw, seq, final, tm=512, tf=256):
    n = h.shape[0]
    body = functools.partial(_ffn_body, tm=tm, tf=tf, blocks_per_seq=seq // tm, final=final)
    hb = tm // HALO
    return pl.pallas_call(
        body,
        grid=(n // tm,),
        in_specs=[
            pl.BlockSpec((tm, D_MODEL), lambda i: (i, 0)),
            pl.BlockSpec((HALO, D_MODEL), lambda i: (jnp.maximum(i * hb - 1, 0), 0)),
            pl.BlockSpec((1, D_MODEL), lambda i: (0, 0)),
            pl.BlockSpec((D_MODEL, 2 * D_FF), lambda i: (0, 0)),
            pl.BlockSpec((FFN_CONV, 2 * D_FF), lambda i: (0, 0)),
            pl.BlockSpec((1, 2 * D_FF), lambda i: (0, 0)),
            pl.BlockSpec((D_FF, D_MODEL), lambda i: (0, 0)),
            pl.BlockSpec((1, D_MODEL), lambda i: (0, 0)),
        ],
        out_specs=pl.BlockSpec((tm, D_MODEL), lambda i: (i, 0)),
        out_shape=jax.ShapeDtypeStruct((n, D_MODEL), F32),
        scratch_shapes=[pltpu.VMEM((HALO + tm, D_MODEL), BF16)],
        compiler_params=_cparams(("arbitrary",)),
        name="conv_ffn",
    )(h, h, nw, wup, cw, cb, wdn, fw)


def _prep_w_in(w_in):
    scale = jnp.concatenate([jnp.full((ATT_WIDTH,), ATT_QK_DIM ** -0.5, F32),
                             jnp.ones((D_IN - ATT_WIDTH,), F32)])
    w = w_in * scale
    w = jnp.pad(w, ((0, 0), (0, 0), (0, D_IN_PAD - D_IN)))
    return w.astype(BF16)


def kernel(x, attn_norm_w, w_in, diff_lambda, diff_subln_w, hgrn_lb_logits, hgrn_norm_w, dn_conv_w,
           dn_A_log, dn_dt_bias, dn_norm_w, w_out, ffn_norm_w, ffn_w_up, ffn_conv_w, ffn_conv_b,
           ffn_w_down, final_norm_w):
    batch, seq, _ = x.shape
    h = x.reshape(batch * seq, D_MODEL)
    w_in_p = _prep_w_in(w_in)
    w_out_b = w_out.astype(BF16)
    w_up_b = ffn_w_up.astype(BF16)
    w_dn_b = ffn_w_down.astype(BF16)
    dn_par = jnp.zeros((DEPTH, 8, LANES), F32)
    dn_par = dn_par.at[:, 0, :HEADS].set(dn_A_log).at[:, 1, :HEADS].set(dn_dt_bias)
    hg_nw = jnp.tile(hgrn_norm_w, (1, HEADS))
    dn_nw = jnp.tile(dn_norm_w, (1, HEADS))
    for l in range(DEPTH):
        lambda_init = 0.8 - 0.6 * math.exp(-0.3 * l)
        att, rest = _inproj(h, attn_norm_w[l][None], w_in_p[l])
        att_o = _attention(att, diff_lambda[l], diff_subln_w[l][None], batch, seq, lambda_init)
        hg_o = _hgrn(rest, hgrn_lb_logits, hg_nw[l][None], batch, seq, l)
        dn_o = _deltanet(rest, dn_conv_w[l], dn_par[l], dn_nw[l][None], batch, seq)
        h = _outproj(att_o, hg_o, dn_o, h, w_out_b[l])
        h = _ffn(h, ffn_norm_w[l][None], w_up_b[l], ffn_conv_w[l], ffn_conv_b[l][None], w_dn_b[l],
                 final_norm_w[None], seq, final=(l == DEPTH - 1))
    return h.reshape(batch, seq, D_MODEL)
```

```python
import functools
import math

import jax
import jax.numpy as jnp
from jax import lax
from jax.experimental import pallas as pl
from jax.experimental.pallas import tpu as pltpu

F32 = jnp.float32
BF16 = jnp.bfloat16
HIGHEST = lax.Precision.HIGHEST

D_MODEL = 1024
DEPTH = 2
ATT_QK_DIM = 64
ATT_V_DIM = 128
ATT_HEADS = 4
ATT_WIDTH = 512
HEADS = 4
HEAD_DIM = 64
REC_WIDTH = HEADS * HEAD_DIM
DN_CONV = 4
DN_CONV_CH = 3 * REC_WIDTH
FFN_CONV = 3
D_FF = 2816
HG_CHUNK = 16
DN_CHUNK = 64
EPS = 1e-6
MASK_NEG = -1e30
F_FLOOR = 1e-30
LOG2E = math.log2(math.e)

ATT_COLS = 3 * ATT_WIDTH
HG_COLS = 4 * REC_WIDTH
DN_COLS = 4 * REC_WIDTH
AB_COLS = 2 * REC_WIDTH
REST_COLS = HG_COLS + DN_COLS + AB_COLS
D_IN = ATT_COLS + HG_COLS + DN_COLS + 2 * HEADS
D_IN_WIDE = ATT_COLS + REST_COLS

VMEM_LIMIT = 56 * 1024 * 1024
LANES = 128
HALO = 16


def _cparams(sem):
    return pltpu.CompilerParams(dimension_semantics=sem, vmem_limit_bytes=VMEM_LIMIT)


def _const_spec(shape):
    return pl.BlockSpec(shape, lambda *_: (0,) * len(shape), pipeline_mode=pl.Buffered(1))


def _rms(x, w):
    return x * lax.rsqrt(jnp.mean(x * x, axis=-1, keepdims=True) + EPS) * w


def _sigmoid(x):
    return 1.0 / (1.0 + jnp.exp(-x))


def _silu(x):
    return x * _sigmoid(x)


def _dot(a, b, precision=None):
    return jnp.dot(a, b, preferred_element_type=F32, precision=precision)


def _head_of(idx):
    return lax.shift_right_logical(idx, 6)


def _same_head(shape):
    r = lax.broadcasted_iota(jnp.int32, shape, 0)
    c = lax.broadcasted_iota(jnp.int32, shape, 1)
    return _head_of(r) == _head_of(c)


def _cumsum_rows(incl, x):
    hi = x.astype(BF16)
    r1 = x - hi.astype(F32)
    mid = r1.astype(BF16)
    lo = (r1 - mid.astype(F32)).astype(BF16)
    return _dot(incl, hi) + _dot(incl, mid) + _dot(incl, lo)


def _chunk_incl(rows, chunk_shift):
    r = lax.broadcasted_iota(jnp.int32, (rows, rows), 0)
    c = lax.broadcasted_iota(jnp.int32, (rows, rows), 1)
    same = lax.shift_right_logical(r, chunk_shift) == lax.shift_right_logical(c, chunk_shift)
    return jnp.where(same & (c <= r), 1.0, 0.0).astype(BF16)


def _inproj_body(x_ref, nw_ref, w_ref, att_ref, rest_ref):
    u = _rms(x_ref[...], nw_ref[...]).astype(BF16)
    step = 512
    for c0 in range(0, ATT_COLS, step):
        att_ref[:, c0:c0 + step] = _dot(u, w_ref[:, c0:c0 + step]).astype(BF16)
    for c0 in range(0, REST_COLS, step):
        rest_ref[:, c0:c0 + step] = _dot(u, w_ref[:, ATT_COLS + c0:ATT_COLS + c0 + step])


def _inproj(h, nw, w, tm=512):
    n = h.shape[0]
    return pl.pallas_call(
        _inproj_body,
        grid=(n // tm,),
        in_specs=[
            pl.BlockSpec((tm, D_MODEL), lambda i: (i, 0)),
            _const_spec((1, D_MODEL)),
            _const_spec((D_MODEL, D_IN_WIDE)),
        ],
        out_specs=[
            pl.BlockSpec((tm, ATT_COLS), lambda i: (i, 0)),
            pl.BlockSpec((tm, REST_COLS), lambda i: (i, 0)),
        ],
        out_shape=[
            jax.ShapeDtypeStruct((n, ATT_COLS), BF16),
            jax.ShapeDtypeStruct((n, REST_COLS), F32),
        ],
        compiler_params=_cparams(("arbitrary",)),
        name="inproj",
    )(h, nw, w)


def _attn_body(lamp_ref, subw_ref, q_ref, k_ref, v_ref, o_ref,
               vt_scr, qt_scr, sa_scr, sb_scr, pa_scr, pb_scr, m_scr, l_scr, al_scr, acc_scr,
               *, tq, lambda_init):
    i = pl.program_id(2)
    nblk = vt_scr.shape[0] - 1

    @pl.when(i == 0)
    def _():
        for c in range(nblk):
            vt_scr[c] = v_ref[c * tq:(c + 1) * tq, :].astype(F32).T.astype(BF16)
        vt_scr[nblk] = jnp.zeros(vt_scr.shape[1:], BF16)
        pb_scr[...] = jnp.zeros(pb_scr.shape, BF16)

    qt = q_ref[...].astype(F32).T * LOG2E
    first = lax.broadcasted_iota(jnp.int32, qt.shape, 0) < ATT_QK_DIM
    qt_scr[:, :tq] = jnp.where(first, qt, 0.0).astype(BF16)
    qt_scr[:, tq:] = jnp.where(first, 0.0, qt).astype(BF16)
    m_scr[...] = jnp.full(m_scr.shape, MASK_NEG, F32)
    l_scr[...] = jnp.zeros(l_scr.shape, F32)
    al_scr[...] = jnp.ones(al_scr.shape, F32)
    acc_scr[...] = jnp.zeros(acc_scr.shape, F32)

    def scores(j):
        kj = k_ref[pl.ds(pl.multiple_of(j * tq, tq), tq), :]
        return _dot(kj, qt_scr[...])

    def phase(j, s_cur, p_cur, s_nxt, p_prv, masked, prefetch=True):
        if prefetch:
            s_nxt[...] = scores(j + 1)
        pv = _dot(vt_scr[jnp.where(j == 0, nblk, j - 1)], p_prv[...])
        s = s_cur[...]
        if masked:
            kpos = lax.broadcasted_iota(jnp.int32, s.shape, 0)
            qpos = lax.broadcasted_iota(jnp.int32, s.shape, 1)
            qpos = jnp.where(qpos >= tq, qpos - tq, qpos)
            s = jnp.where(kpos <= qpos, s, MASK_NEG)
        m_old = m_scr[...]
        m_new = jnp.maximum(m_old, jnp.max(s, axis=0, keepdims=True))
        alpha = jnp.exp2(m_old - m_new)
        p = jnp.exp2(s - m_new)
        l_scr[...] = alpha * l_scr[...] + jnp.sum(p, axis=0, keepdims=True)
        m_scr[...] = m_new
        p_cur[...] = p.astype(BF16)
        acc_scr[...] = al_scr[...] * acc_scr[...] + pv
        al_scr[...] = alpha

    def finish(j, p_cur):
        acc_scr[...] = al_scr[...] * acc_scr[...] + _dot(vt_scr[j], p_cur[...])

    sa_scr[...] = scores(0)

    def pair(jj, carry):
        phase(2 * jj, sa_scr, pa_scr, sb_scr, pb_scr, False)
        phase(2 * jj + 1, sb_scr, pb_scr, sa_scr, pa_scr, False)
        return carry

    lax.fori_loop(0, lax.shift_right_logical(i, 1), pair, 0)

    @pl.when((i & 1) == 0)
    def _():
        phase(i, sa_scr, pa_scr, sb_scr, pb_scr, True, prefetch=False)
        finish(i, pa_scr)

    @pl.when((i & 1) == 1)
    def _():
        phase(i - 1, sa_scr, pa_scr, sb_scr, pb_scr, False)
        phase(i, sb_scr, pb_scr, sa_scr, pa_scr, True, prefetch=False)
        finish(i, pb_scr)

    acc = acc_scr[...]
    lp = lamp_ref[...]
    lam = (jnp.exp(jnp.sum(lp[0:1] * lp[1:2], axis=-1, keepdims=True))
           - jnp.exp(jnp.sum(lp[2:3] * lp[3:4], axis=-1, keepdims=True)) + lambda_init)
    rl = 1.0 / l_scr[...]
    ot = acc[:, :tq] * rl[:, :tq] - lam * (acc[:, tq:] * rl[:, tq:])
    o = _rms(ot.T, subw_ref[...]) * (1.0 - lambda_init)
    o_ref[...] = o.astype(BF16)


def _attention(att, lamp, subw, batch, seq, lambda_init, tq=512):
    n = att.shape[0]
    nq = seq // tq
    body = functools.partial(_attn_body, tq=tq, lambda_init=lambda_init)
    return pl.pallas_call(
        body,
        grid=(batch, ATT_HEADS, nq),
        in_specs=[
            pl.BlockSpec((4, ATT_QK_DIM), lambda b, h, i: (0, 0)),
            pl.BlockSpec((1, ATT_V_DIM), lambda b, h, i: (0, 0)),
            pl.BlockSpec((tq, LANES), lambda b, h, i: (b * nq + i, h)),
            pl.BlockSpec((seq, LANES), lambda b, h, i: (b, ATT_HEADS + h)),
            pl.BlockSpec((seq, LANES), lambda b, h, i: (b, 2 * ATT_HEADS + h)),
        ],
        out_specs=pl.BlockSpec((tq, LANES), lambda b, h, i: (b * nq + i, h)),
        out_shape=jax.ShapeDtypeStruct((n, ATT_WIDTH), BF16),
        scratch_shapes=[
            pltpu.VMEM((nq + 1, LANES, tq), BF16),
            pltpu.VMEM((LANES, 2 * tq), BF16),
            pltpu.VMEM((tq, 2 * tq), F32),
            pltpu.VMEM((tq, 2 * tq), F32),
            pltpu.VMEM((tq, 2 * tq), BF16),
            pltpu.VMEM((tq, 2 * tq), BF16),
            pltpu.VMEM((1, 2 * tq), F32),
            pltpu.VMEM((1, 2 * tq), F32),
            pltpu.VMEM((1, 2 * tq), F32),
            pltpu.VMEM((LANES, 2 * tq), F32),
        ],
        compiler_params=_cparams(("arbitrary", "arbitrary", "arbitrary")),
        name="diff_attention",
    )(lamp, subw, att, att, att)


def _hgrn_body(x_ref, lbl_ref, nw_ref, o_ref, st_scr, b_scr, q_scr, k_scr, v_scr,
               qd_scr, oi_scr, *, layer, rows):
    @pl.when(pl.program_id(1) == 0)
    def _():
        st_scr[...] = jnp.zeros(st_scr.shape, F32)

    lg = lbl_ref[...]
    e = jnp.exp(lg - jnp.max(lg, axis=0, keepdims=True))
    sm = e / jnp.sum(e, axis=0, keepdims=True)
    lb = jnp.zeros((1, REC_WIDTH), F32)
    for i in range(1, layer + 1):
        lb = lb + sm[i:i + 1]

    x = x_ref[...]
    q = x[:, 0:REC_WIDTH]
    fp = x[:, REC_WIDTH:2 * REC_WIDTH]
    iv = x[:, 2 * REC_WIDTH:3 * REC_WIDTH]
    gate = x[:, 3 * REC_WIDTH:4 * REC_WIDTH]
    qf = _silu(q)
    f = lb + (1.0 - lb) * _sigmoid(fp)
    logf = jnp.log(jnp.maximum(f, F_FLOOR))
    kf = (1.0 - lb) * _sigmoid(-fp)

    b = _cumsum_rows(_chunk_incl(rows, 4), logf)
    b_scr[...] = b
    q_scr[...] = qf
    k_scr[...] = kf
    v_scr[...] = iv
    qd_scr[...] = qf * jnp.exp(b)

    same = _same_head((REC_WIDTH, REC_WIDTH))
    ind = jnp.where(same, 1.0, 0.0).astype(BF16)
    trow = lax.broadcasted_iota(jnp.int32, (HG_CHUNK, REC_WIDTH), 0)

    def chunk(c, carry):
        base = pl.multiple_of(c * HG_CHUNK, HG_CHUNK)
        sl = pl.ds(base, HG_CHUNK)
        b_c = b_scr[sl, :]
        q_c = q_scr[sl, :]
        b_last = b_scr[pl.ds(base + HG_CHUNK - 1, 1), :]
        kd_c = k_scr[sl, :] * jnp.exp(b_last - b_c)
        st = st_scr[...]
        o_inter = lax.dot_general(qd_scr[sl, :].astype(BF16), st.astype(BF16),
                                  (((1,), (1,)), ((), ())), preferred_element_type=F32)
        kvt = lax.dot_general(v_scr[sl, :].astype(BF16), kd_c.astype(BF16),
                              (((0,), (0,)), ((), ())), preferred_element_type=F32)
        st_scr[...] = st * jnp.exp(b_last) + jnp.where(same, kvt, 0.0)
        slabs = []
        for s in range(HG_CHUNK):
            row = pl.ds(base + s, 1)
            rel = jnp.where(trow >= s, b_c - b_scr[row, :], MASK_NEG)
            slabs.append((jnp.exp(rel) * q_c * k_scr[row, :]).astype(BF16))
        a = _dot(jnp.concatenate(slabs, axis=0), ind)
        o_intra = a[0:HG_CHUNK] * v_scr[pl.ds(base, 1), :]
        for s in range(1, HG_CHUNK):
            o_intra = o_intra + a[s * HG_CHUNK:(s + 1) * HG_CHUNK] * v_scr[pl.ds(base + s, 1), :]
        oi_scr[sl, :] = o_inter + o_intra
        return carry

    lax.fori_loop(0, rows // HG_CHUNK, chunk, 0, unroll=4)

    o = oi_scr[...]
    ms = _dot((o * o).astype(BF16), ind) * (1.0 / HEAD_DIM)
    y = o * lax.rsqrt(ms + EPS) * nw_ref[...] * _silu(gate)
    o_ref[...] = y.astype(BF16)


def _hgrn(rest, lb_logits, nw, batch, seq, layer, rows=256):
    n = rest.shape[0]
    nb = seq // rows
    body = functools.partial(_hgrn_body, layer=layer, rows=rows)
    rec = pltpu.VMEM((rows, REC_WIDTH), F32)
    return pl.pallas_call(
        body,
        grid=(batch, nb),
        in_specs=[
            pl.BlockSpec((rows, HG_COLS), lambda b, r: (b * nb + r, 0)),
            pl.BlockSpec((DEPTH, REC_WIDTH), lambda b, r: (0, 0)),
            pl.BlockSpec((1, REC_WIDTH), lambda b, r: (0, 0)),
        ],
        out_specs=pl.BlockSpec((rows, REC_WIDTH), lambda b, r: (b * nb + r, 0)),
        out_shape=jax.ShapeDtypeStruct((n, REC_WIDTH), BF16),
        scratch_shapes=[pltpu.VMEM((REC_WIDTH, REC_WIDTH), F32)] + [rec] * 6,
        compiler_params=_cparams(("arbitrary", "arbitrary")),
        name="hgrn2",
    )(rest, lb_logits, nw)


def _dn_body(x_ref, a_ref, b_ref, cw_ref, alog_ref, dtb_ref, nw_ref, o_ref,
             carry_scr, xs_scr, s_scr, *, rows):
    nbatch = x_ref.shape[0]
    nchunk = rows // DN_CHUNK
    flat = nbatch * rows

    @pl.when(pl.program_id(0) == 0)
    def _():
        carry_scr[...] = jnp.zeros(carry_scr.shape, F32)
        s_scr[...] = jnp.zeros(s_scr.shape, F32)

    cw = cw_ref[...]
    ys = []
    for bi in range(nbatch):
        xs_scr[bi, 0:8, :] = carry_scr[bi]
        xs_scr[bi, 8:, :] = x_ref[bi, :, 0:DN_CONV_CH]
        carry_scr[bi] = x_ref[bi, rows - 8:rows, 0:DN_CONV_CH]
        y = cw[0:1] * xs_scr[bi, 5:5 + rows, :]
        for j in range(1, DN_CONV):
            y = y + cw[j:j + 1] * xs_scr[bi, 5 + j:5 + j + rows, :]
        ys.append(y)
    y = _silu(jnp.concatenate(ys, axis=0))
    z = x_ref[...].reshape(flat, DN_COLS)[:, DN_CONV_CH:]
    qc = y[:, 0:REC_WIDTH]
    kc = y[:, REC_WIDTH:2 * REC_WIDTH]
    vc = y[:, 2 * REC_WIDTH:]

    same = _same_head((REC_WIDTH, REC_WIDTH))
    ind = jnp.where(same, 1.0, 0.0).astype(BF16)
    qn = qc * lax.rsqrt(_dot((qc * qc).astype(BF16), ind) + EPS) * (HEAD_DIM ** -0.5)
    kn = kc * lax.rsqrt(_dot((kc * kc).astype(BF16), ind) + EPS)

    sp_in = a_ref[...].reshape(flat, REC_WIDTH) + dtb_ref[...]
    softplus = jnp.maximum(sp_in, 0.0) + jnp.log1p(jnp.exp(-jnp.abs(sp_in)))
    g = -jnp.exp(alog_ref[...]) * softplus
    beta = _sigmoid(b_ref[...].reshape(flat, REC_WIDTH))
    kb = kn * beta
    vb = vc * beta

    crow = lax.broadcasted_iota(jnp.int32, (DN_CHUNK, REC_WIDTH), 0)
    slane = lax.broadcasted_iota(jnp.int32, (DN_CHUNK, REC_WIDTH), 1) & (HEAD_DIM - 1)
    diag = crow == slane
    eye = jnp.where(diag, 1.0, 0.0).astype(F32)
    incl = _chunk_incl(DN_CHUNK, 6)

    def bdiag(v):
        v16 = v.astype(BF16)
        return jnp.where(same, jnp.concatenate([v16] * HEADS, axis=0), jnp.zeros((), BF16))

    chunks = [(bi, n) for n in range(nchunk) for bi in range(nbatch)]
    rows_of = {}
    for bi, n in chunks:
        start = bi * rows + n * DN_CHUNK
        rows_of[(bi, n)] = slice(start, start + DN_CHUNK)
    gc, g_row, kt_bd, a_kk, a_qk, xinv, pw, u, w = {}, {}, {}, {}, {}, {}, {}, {}, {}
    for ch in chunks:
        gc[ch] = _cumsum_rows(incl, g[rows_of[ch]])
    for ch in chunks:
        g_row[ch] = jnp.sum(jnp.where(diag, gc[ch], 0.0), axis=0, keepdims=True)
        kt = jnp.concatenate([kn[rows_of[ch]]] * HEADS, axis=0).T
        kt_bd[ch] = jnp.where(same, kt, 0.0)
    for ch in chunks:
        sl = rows_of[ch]
        decay = jnp.exp(jnp.where(slane <= crow, gc[ch] - g_row[ch], MASK_NEG))
        sc = _dot(jnp.concatenate([kb[sl], qn[sl]], axis=0).astype(BF16), kt_bd[ch].astype(BF16))
        a_kk[ch] = jnp.where(slane < crow, sc[:DN_CHUNK] * decay, 0.0)
        a_qk[ch] = sc[DN_CHUNK:] * decay
    for ch in chunks:
        xinv[ch] = eye - a_kk[ch]
        pw[ch] = _dot(a_kk[ch].astype(BF16), bdiag(a_kk[ch]))
    for _ in range(4):
        for ch in chunks:
            r = _dot(jnp.concatenate([xinv[ch], pw[ch]], axis=0).astype(BF16), bdiag(pw[ch]))
            xinv[ch] = xinv[ch] + r[:DN_CHUNK]
            pw[ch] = r[DN_CHUNK:]
    for ch in chunks:
        sl = rows_of[ch]
        xi = (xinv[ch] + _dot(xinv[ch].astype(BF16), bdiag(pw[ch]))).astype(BF16)
        u[ch] = _dot(xi, bdiag(vb[sl]))
        w[ch] = _dot(xi, bdiag(kb[sl] * jnp.exp(gc[ch])))
    outs = {}
    state = [s_scr[bi] for bi in range(nbatch)]
    for n in range(nchunk):
        for bi in range(nbatch):
            ch = (bi, n)
            q_dec = qn[rows_of[ch]] * jnp.exp(gc[ch])
            r1 = _dot(jnp.concatenate([w[ch], q_dec], axis=0).astype(BF16), state[bi].astype(BF16))
            v_new = u[ch] - r1[:DN_CHUNK]
            gl_row = gc[ch][DN_CHUNK - 1:DN_CHUNK]
            kdt_bd = kt_bd[ch] * jnp.exp(gl_row - g_row[ch])
            r2 = _dot(jnp.concatenate([a_qk[ch], kdt_bd], axis=0).astype(BF16), bdiag(v_new))
            outs[ch] = r1[DN_CHUNK:] + r2[:DN_CHUNK]
            state[bi] = state[bi] * jnp.exp(gl_row) + r2[DN_CHUNK:]
    for bi in range(nbatch):
        s_scr[bi] = state[bi]

    o = jnp.concatenate([outs[(bi, n)] for bi in range(nbatch) for n in range(nchunk)], axis=0)
    ms = _dot((o * o).astype(BF16), ind) * (1.0 / HEAD_DIM)
    o = o * lax.rsqrt(ms + EPS) * nw_ref[...] * _silu(z)
    o_ref[...] = o.reshape(nbatch, rows, REC_WIDTH).astype(BF16)


def _deltanet(rest, cw, alog, dtb, nw, batch, seq, rows=128):
    rest3 = rest.reshape(batch, seq, REST_COLS)
    body = functools.partial(_dn_body, rows=rows)
    ab0 = (HG_COLS + DN_COLS) // REC_WIDTH
    out = pl.pallas_call(
        body,
        grid=(seq // rows,),
        in_specs=[
            pl.BlockSpec((batch, rows, DN_COLS), lambda r: (0, r, HG_COLS // DN_COLS)),
            pl.BlockSpec((batch, rows, REC_WIDTH), lambda r: (0, r, ab0)),
            pl.BlockSpec((batch, rows, REC_WIDTH), lambda r: (0, r, ab0 + 1)),
            pl.BlockSpec((DN_CONV, DN_CONV_CH), lambda r: (0, 0)),
            pl.BlockSpec((1, REC_WIDTH), lambda r: (0, 0)),
            pl.BlockSpec((1, REC_WIDTH), lambda r: (0, 0)),
            pl.BlockSpec((1, REC_WIDTH), lambda r: (0, 0)),
        ],
        out_specs=pl.BlockSpec((batch, rows, REC_WIDTH), lambda r: (0, r, 0)),
        out_shape=jax.ShapeDtypeStruct((batch, seq, REC_WIDTH), BF16),
        scratch_shapes=[
            pltpu.VMEM((batch, 8, DN_CONV_CH), F32),
            pltpu.VMEM((batch, rows + 8, DN_CONV_CH), F32),
            pltpu.VMEM((batch, REC_WIDTH, REC_WIDTH), F32),
        ],
        compiler_params=_cparams(("arbitrary",)),
        name="gated_deltanet",
    )(rest3, rest3, rest3, cw, alog, dtb, nw)
    return out.reshape(batch * seq, REC_WIDTH)


def _mix_ffn_body(att_ref, hg_ref, dn_ref, h_ref, wo_ref, nw_ref, wup_ref, cw_ref, cb_ref, wdn_ref,
                  fw_ref, o_ref, u_scr, act_scr, *, tm, tf, blocks_per_seq, final):
    first = pl.program_id(0) % blocks_per_seq == 0

    @pl.when(first)
    def _():
        u_scr[:HALO, :] = jnp.zeros((HALO, D_MODEL), BF16)

    @pl.when(jnp.logical_not(first))
    def _():
        u_scr[:HALO, :] = u_scr[tm:tm + HALO, :]

    h1 = h_ref[...] + _dot(att_ref[...], wo_ref[0:ATT_WIDTH, :])
    h1 = h1 + _dot(hg_ref[...], wo_ref[ATT_WIDTH:ATT_WIDTH + REC_WIDTH, :])
    h1 = h1 + _dot(dn_ref[...], wo_ref[ATT_WIDTH + REC_WIDTH:, :])
    o_ref[...] = h1
    u_scr[HALO:, :] = _rms(h1, nw_ref[...]).astype(BF16)
    u = u_scr[...]

    def conv(zz, w, b):
        lo = HALO - (FFN_CONV - 1)
        out = b
        for j in range(FFN_CONV):
            out = out + w[j:j + 1] * zz[lo + j:lo + j + tm]
        return out

    for j in range(D_FF // tf):
        gs = slice(j * tf, (j + 1) * tf)
        vs = slice(D_FF + j * tf, D_FF + (j + 1) * tf)
        gate = conv(_dot(u, wup_ref[:, gs]), cw_ref[:, gs], cb_ref[:, gs])
        val = conv(_dot(u, wup_ref[:, vs]), cw_ref[:, vs], cb_ref[:, vs])
        act_scr[:, gs] = (_silu(gate) * val).astype(BF16)

    act = act_scr[...]
    nstep = 256
    for c0 in range(0, D_MODEL, nstep):
        y = o_ref[:, c0:c0 + nstep] + _dot(act, wdn_ref[:, c0:c0 + nstep])
        o_ref[:, c0:c0 + nstep] = y
    if final:
        o_ref[...] = _rms(o_ref[...], fw_ref[...])


def _mix_ffn(att_o, hg_o, dn_o, h, wo, nw, wup, cw, cb, wdn, fw, seq, final, tm=512, tf=256):
    n = h.shape[0]
    body = functools.partial(_mix_ffn_body, tm=tm, tf=tf, blocks_per_seq=seq // tm, final=final)
    return pl.pallas_call(
        body,
        grid=(n // tm,),
        in_specs=[
            pl.BlockSpec((tm, ATT_WIDTH), lambda i: (i, 0)),
            pl.BlockSpec((tm, REC_WIDTH), lambda i: (i, 0)),
            pl.BlockSpec((tm, REC_WIDTH), lambda i: (i, 0)),
            pl.BlockSpec((tm, D_MODEL), lambda i: (i, 0)),
            _const_spec((D_MODEL, D_MODEL)),
            _const_spec((1, D_MODEL)),
            _const_spec((D_MODEL, 2 * D_FF)),
            _const_spec((FFN_CONV, 2 * D_FF)),
            _const_spec((1, 2 * D_FF)),
            _const_spec((D_FF, D_MODEL)),
            _const_spec((1, D_MODEL)),
        ],
        out_specs=pl.BlockSpec((tm, D_MODEL), lambda i: (i, 0)),
        out_shape=jax.ShapeDtypeStruct((n, D_MODEL), F32),
        scratch_shapes=[
            pltpu.VMEM((HALO + tm, D_MODEL), BF16),
            pltpu.VMEM((tm, D_FF), BF16),
        ],
        compiler_params=_cparams(("arbitrary",)),
        name="mix_ffn",
    )(att_o, hg_o, dn_o, h, wo, nw, wup, cw, cb, wdn, fw)


def _prep_w_in(w_in):
    scale = jnp.concatenate([jnp.full((ATT_WIDTH,), ATT_QK_DIM ** -0.5, F32),
                             jnp.ones((D_IN - ATT_WIDTH,), F32)])
    w = w_in * scale
    main = w[..., :D_IN - 2 * HEADS]
    ab = jnp.repeat(w[..., D_IN - 2 * HEADS:], HEAD_DIM, axis=-1)
    return jnp.concatenate([main, ab], axis=-1).astype(BF16)


def kernel(x, attn_norm_w, w_in, diff_lambda, diff_subln_w, hgrn_lb_logits, hgrn_norm_w, dn_conv_w,
           dn_A_log, dn_dt_bias, dn_norm_w, w_out, ffn_norm_w, ffn_w_up, ffn_conv_w, ffn_conv_b,
           ffn_w_down, final_norm_w):
    batch, seq, _ = x.shape
    h = x.reshape(batch * seq, D_MODEL)
    w_in_p = _prep_w_in(w_in)
    w_out_b = w_out.astype(BF16)
    w_up_b = ffn_w_up.astype(BF16)
    w_dn_b = ffn_w_down.astype(BF16)
    dn_alog = jnp.repeat(dn_A_log, HEAD_DIM, axis=-1)
    dn_dtb = jnp.repeat(dn_dt_bias, HEAD_DIM, axis=-1)
    hg_nw = jnp.tile(hgrn_norm_w, (1, HEADS))
    dn_nw = jnp.tile(dn_norm_w, (1, HEADS))
    for l in range(DEPTH):
        lambda_init = 0.8 - 0.6 * math.exp(-0.3 * l)
        att, rest = _inproj(h, attn_norm_w[l][None], w_in_p[l])
        att_o = _attention(att, diff_lambda[l], diff_subln_w[l][None], batch, seq, lambda_init)
        hg_o = _hgrn(rest, hgrn_lb_logits, hg_nw[l][None], batch, seq, l)
        dn_o = _deltanet(rest, dn_conv_w[l], dn_alog[l][None], dn_dtb[l][None], dn_nw[l][None],
                         batch, seq)
        h = _mix_ffn(att_o, hg_o, dn_o, h, w_out_b[l], ffn_norm_w[l][None], w_up_b[l], ffn_conv_w[l],
                     ffn_conv_b[l][None], w_dn_b[l], final_norm_w[None], seq, final=(l == DEPTH - 1))
    return h.reshape(batch, seq, D_MODEL)
```

```python
import functools
import math

import jax
import jax.numpy as jnp
from jax import lax
from jax.experimental import pallas as pl
from jax.experimental.pallas import tpu as pltpu

F32 = jnp.float32
BF16 = jnp.bfloat16
HIGHEST = lax.Precision.HIGHEST

D_MODEL = 1024
DEPTH = 2
ATT_QK_DIM = 64
ATT_V_DIM = 128
ATT_HEADS = 4
ATT_WIDTH = 512
HEADS = 4
HEAD_DIM = 64
REC_WIDTH = HEADS * HEAD_DIM
DN_CONV = 4
DN_CONV_CH = 3 * REC_WIDTH
FFN_CONV = 3
D_FF = 2816
HG_CHUNK = 16
DN_CHUNK = 64
EPS = 1e-6
MASK_NEG = -1e30
F_FLOOR = 1e-30
LOG2E = math.log2(math.e)

ATT_COLS = 3 * ATT_WIDTH
HG_COLS = 4 * REC_WIDTH
DN_COLS = 4 * REC_WIDTH
AB_COLS = 2 * REC_WIDTH
REST_COLS = HG_COLS + DN_COLS + AB_COLS
D_IN = ATT_COLS + HG_COLS + DN_COLS + 2 * HEADS
D_IN_WIDE = ATT_COLS + REST_COLS

VMEM_LIMIT = 56 * 1024 * 1024
LANES = 128
HALO = 16


def _cparams(sem, flags=None):
    return pltpu.CompilerParams(dimension_semantics=sem, vmem_limit_bytes=VMEM_LIMIT, flags=flags)


def _const_spec(shape):
    return pl.BlockSpec(shape, lambda *_: (0,) * len(shape), pipeline_mode=pl.Buffered(1))


def _rms(x, w):
    return x * lax.rsqrt(jnp.mean(x * x, axis=-1, keepdims=True) + EPS) * w


def _sigmoid(x):
    return 1.0 / (1.0 + jnp.exp(-x))


def _silu(x):
    return x * _sigmoid(x)


def _dot(a, b, precision=None):
    return jnp.dot(a, b, preferred_element_type=F32, precision=precision)


def _head_of(idx):
    return lax.shift_right_logical(idx, 6)


def _same_head(shape):
    r = lax.broadcasted_iota(jnp.int32, shape, 0)
    c = lax.broadcasted_iota(jnp.int32, shape, 1)
    return _head_of(r) == _head_of(c)


def _cumsum_rows(incl, x):
    hi = x.astype(BF16)
    r1 = x - hi.astype(F32)
    mid = r1.astype(BF16)
    lo = (r1 - mid.astype(F32)).astype(BF16)
    return _dot(incl, hi) + _dot(incl, mid) + _dot(incl, lo)


def _chunk_incl(rows, chunk_shift):
    r = lax.broadcasted_iota(jnp.int32, (rows, rows), 0)
    c = lax.broadcasted_iota(jnp.int32, (rows, rows), 1)
    same = lax.shift_right_logical(r, chunk_shift) == lax.shift_right_logical(c, chunk_shift)
    return jnp.where(same & (c <= r), 1.0, 0.0).astype(BF16)


def _inproj_body(x_ref, nw_ref, w_ref, att_ref, rest_ref):
    u = _rms(x_ref[...], nw_ref[...]).astype(BF16)
    step = 512
    for c0 in range(0, ATT_COLS, step):
        att_ref[:, c0:c0 + step] = _dot(u, w_ref[:, c0:c0 + step]).astype(BF16)
    for c0 in range(0, REST_COLS, step):
        rest_ref[:, c0:c0 + step] = _dot(u, w_ref[:, ATT_COLS + c0:ATT_COLS + c0 + step])


def _inproj(h, nw, w, tm=512):
    n = h.shape[0]
    return pl.pallas_call(
        _inproj_body,
        grid=(n // tm,),
        in_specs=[
            pl.BlockSpec((tm, D_MODEL), lambda i: (i, 0)),
            _const_spec((1, D_MODEL)),
            _const_spec((D_MODEL, D_IN_WIDE)),
        ],
        out_specs=[
            pl.BlockSpec((tm, ATT_COLS), lambda i: (i, 0)),
            pl.BlockSpec((tm, REST_COLS), lambda i: (i, 0)),
        ],
        out_shape=[
            jax.ShapeDtypeStruct((n, ATT_COLS), BF16),
            jax.ShapeDtypeStruct((n, REST_COLS), F32),
        ],
        compiler_params=_cparams(("arbitrary",)),
        name="inproj",
    )(h, nw, w)


def _attn_body(lamp_ref, subw_ref, q_ref, k_ref, v_ref, o_ref,
               vt_scr, qt_scr, sa_scr, sb_scr, ma_scr, mb_scr, pa_scr, pb_scr, m_scr, l_scr, al_scr,
               acc_scr, *, tq, tk, lambda_init):
    i = pl.program_id(2)
    nblk = vt_scr.shape[0] - 1

    @pl.when(i == 0)
    def _():
        for c in range(nblk):
            vt_scr[c] = v_ref[c * tk:(c + 1) * tk, :].astype(F32).T.astype(BF16)
        vt_scr[nblk] = jnp.zeros(vt_scr.shape[1:], BF16)
        pb_scr[...] = jnp.zeros(pb_scr.shape, BF16)

    qt = q_ref[...].astype(F32).T * LOG2E
    first = lax.broadcasted_iota(jnp.int32, qt.shape, 0) < ATT_QK_DIM
    qt_scr[:, :tq] = jnp.where(first, qt, 0.0).astype(BF16)
    qt_scr[:, tq:] = jnp.where(first, 0.0, qt).astype(BF16)
    m_scr[...] = jnp.full(m_scr.shape, MASK_NEG, F32)
    l_scr[...] = jnp.zeros(l_scr.shape, F32)
    al_scr[...] = jnp.ones(al_scr.shape, F32)
    acc_scr[...] = jnp.zeros(acc_scr.shape, F32)

    def put_scores(dst, j):
        kj = k_ref[pl.ds(pl.multiple_of(j * tk, tk), tk), :]
        s = _dot(kj, qt_scr[...])
        dst[0][...] = s
        dst[1][...] = jnp.max(s, axis=0, keepdims=True)

    def phase(j, s_cur, p_cur, s_nxt, p_prv, masked, prefetch=True):
        if prefetch:
            put_scores(s_nxt, j + 1)
        pv = _dot(vt_scr[jnp.where(j == 0, nblk, j - 1)], p_prv[...])
        s = s_cur[0][...]
        if masked:
            kpos = lax.broadcasted_iota(jnp.int32, s.shape, 0) + (j * tk - i * tq)
            qpos = lax.broadcasted_iota(jnp.int32, s.shape, 1)
            qpos = jnp.where(qpos >= tq, qpos - tq, qpos)
            s = jnp.where(kpos <= qpos, s, MASK_NEG)
            m_tile = jnp.max(s, axis=0, keepdims=True)
        else:
            m_tile = s_cur[1][...]
        m_old = m_scr[...]
        m_new = jnp.maximum(m_old, m_tile)
        alpha = jnp.exp2(m_old - m_new)
        p = jnp.exp2(s - m_new)
        l_scr[...] = alpha * l_scr[...] + jnp.sum(p, axis=0, keepdims=True)
        m_scr[...] = m_new
        p_cur[...] = p.astype(BF16)
        acc_scr[...] = al_scr[...] * acc_scr[...] + pv
        al_scr[...] = alpha

    def finish(j, p_cur):
        acc_scr[...] = al_scr[...] * acc_scr[...] + _dot(vt_scr[j], p_cur[...])

    sa = (sa_scr, ma_scr)
    sb = (sb_scr, mb_scr)
    put_scores(sa, 0)

    def pair(jj, carry):
        phase(2 * jj, sa, pa_scr, sb, pb_scr, False)
        phase(2 * jj + 1, sb, pb_scr, sa, pa_scr, False)
        return carry

    lax.fori_loop(0, lax.shift_right_logical(i, 1), pair, 0)

    @pl.when((i & 1) == 0)
    def _():
        phase(i, sa, pa_scr, sb, pb_scr, True, prefetch=False)
        finish(i, pa_scr)

    @pl.when((i & 1) == 1)
    def _():
        phase(i - 1, sa, pa_scr, sb, pb_scr, False)
        phase(i, sb, pb_scr, sa, pa_scr, True, prefetch=False)
        finish(i, pb_scr)

    acc = acc_scr[...]
    lp = lamp_ref[...]
    lam = (jnp.exp(jnp.sum(lp[0:1] * lp[1:2], axis=-1, keepdims=True))
           - jnp.exp(jnp.sum(lp[2:3] * lp[3:4], axis=-1, keepdims=True)) + lambda_init)
    rl = 1.0 / l_scr[...]
    ot = acc[:, :tq] * rl[:, :tq] - lam * (acc[:, tq:] * rl[:, tq:])
    ot = ot * (lax.rsqrt(jnp.mean(ot * ot, axis=0, keepdims=True) + EPS) * (1.0 - lambda_init))
    o_ref[...] = (ot.T * subw_ref[...]).astype(BF16)


def _attention(att, lamp, subw, batch, seq, lambda_init, tq=512):
    n = att.shape[0]
    nq = seq // tq
    tk = tq
    body = functools.partial(_attn_body, tq=tq, tk=tk, lambda_init=lambda_init)
    return pl.pallas_call(
        body,
        grid=(batch, ATT_HEADS, nq),
        in_specs=[
            pl.BlockSpec((4, ATT_QK_DIM), lambda b, h, i: (0, 0)),
            pl.BlockSpec((1, ATT_V_DIM), lambda b, h, i: (0, 0)),
            pl.BlockSpec((tq, LANES), lambda b, h, i: (b * nq + i, h)),
            pl.BlockSpec((seq, LANES), lambda b, h, i: (b, ATT_HEADS + h)),
            pl.BlockSpec((seq, LANES), lambda b, h, i: (b, 2 * ATT_HEADS + h)),
        ],
        out_specs=pl.BlockSpec((tq, LANES), lambda b, h, i: (b * nq + i, h)),
        out_shape=jax.ShapeDtypeStruct((n, ATT_WIDTH), BF16),
        scratch_shapes=[
            pltpu.VMEM((seq // tk + 1, LANES, tk), BF16),
            pltpu.VMEM((LANES, 2 * tq), BF16),
            pltpu.VMEM((tk, 2 * tq), F32),
            pltpu.VMEM((tk, 2 * tq), F32),
            pltpu.VMEM((1, 2 * tq), F32),
            pltpu.VMEM((1, 2 * tq), F32),
            pltpu.VMEM((tk, 2 * tq), BF16),
            pltpu.VMEM((tk, 2 * tq), BF16),
            pltpu.VMEM((1, 2 * tq), F32),
            pltpu.VMEM((1, 2 * tq), F32),
            pltpu.VMEM((1, 2 * tq), F32),
            pltpu.VMEM((LANES, 2 * tq), F32),
        ],
        compiler_params=_cparams(("arbitrary", "arbitrary", "arbitrary")),
        name="diff_attention",
    )(lamp, subw, att, att, att)


def _hgrn_body(x_ref, lbl_ref, nw_ref, o_ref, st_scr, b_scr, q_scr, k_scr, v_scr,
               qd_scr, oi_scr, *, layer, rows):
    @pl.when(pl.program_id(1) == 0)
    def _():
        st_scr[...] = jnp.zeros(st_scr.shape, F32)

    lg = lbl_ref[...]
    e = jnp.exp(lg - jnp.max(lg, axis=0, keepdims=True))
    sm = e / jnp.sum(e, axis=0, keepdims=True)
    lb = jnp.zeros((1, REC_WIDTH), F32)
    for i in range(1, layer + 1):
        lb = lb + sm[i:i + 1]

    x = x_ref[...]
    q = x[:, 0:REC_WIDTH]
    fp = x[:, REC_WIDTH:2 * REC_WIDTH]
    iv = x[:, 2 * REC_WIDTH:3 * REC_WIDTH]
    gate = x[:, 3 * REC_WIDTH:4 * REC_WIDTH]
    qf = _silu(q)
    f = lb + (1.0 - lb) * _sigmoid(fp)
    logf = jnp.log2(jnp.maximum(f, F_FLOOR))
    kf = (1.0 - lb) * _sigmoid(-fp)

    b = _cumsum_rows(_chunk_incl(rows, 4), logf)
    b_scr[...] = b
    q_scr[...] = qf
    k_scr[...] = kf
    v_scr[...] = iv
    qd_scr[...] = qf * jnp.exp2(b)

    same = _same_head((REC_WIDTH, REC_WIDTH))
    ind = jnp.where(same, 1.0, 0.0).astype(BF16)
    trow = lax.broadcasted_iota(jnp.int32, (HG_CHUNK, REC_WIDTH), 0)

    def chunk(c, carry):
        base = pl.multiple_of(c * HG_CHUNK, HG_CHUNK)
        sl = pl.ds(base, HG_CHUNK)
        b_c = b_scr[sl, :]
        q_c = q_scr[sl, :]
        b_last = b_scr[pl.ds(base + HG_CHUNK - 1, 1), :]
        kd_c = k_scr[sl, :] * jnp.exp2(b_last - b_c)
        st = st_scr[...]
        o_inter = lax.dot_general(qd_scr[sl, :].astype(BF16), st.astype(BF16),
                                  (((1,), (1,)), ((), ())), preferred_element_type=F32)
        kvt = lax.dot_general(v_scr[sl, :].astype(BF16), kd_c.astype(BF16),
                              (((0,), (0,)), ((), ())), preferred_element_type=F32)
        st_scr[...] = st * jnp.exp2(b_last) + jnp.where(same, kvt, 0.0)
        slabs = []
        for s in range(HG_CHUNK):
            row = pl.ds(base + s, 1)
            rel = jnp.where(trow >= s, b_c - b_scr[row, :], MASK_NEG)
            slabs.append((jnp.exp2(rel) * q_c * k_scr[row, :]).astype(BF16))
        a = _dot(jnp.concatenate(slabs, axis=0), ind)
        o_intra = a[0:HG_CHUNK] * v_scr[pl.ds(base, 1), :]
        for s in range(1, HG_CHUNK):
            o_intra = o_intra + a[s * HG_CHUNK:(s + 1) * HG_CHUNK] * v_scr[pl.ds(base + s, 1), :]
        oi_scr[sl, :] = o_inter + o_intra
        return carry

    lax.fori_loop(0, rows // HG_CHUNK, chunk, 0, unroll=4)

    o = oi_scr[...]
    ms = _dot((o * o).astype(BF16), ind) * (1.0 / HEAD_DIM)
    y = o * lax.rsqrt(ms + EPS) * nw_ref[...] * _silu(gate)
    o_ref[...] = y.astype(BF16)


def _hgrn(rest, lb_logits, nw, batch, seq, layer, rows=256):
    n = rest.shape[0]
    nb = seq // rows
    body = functools.partial(_hgrn_body, layer=layer, rows=rows)
    rec = pltpu.VMEM((rows, REC_WIDTH), F32)
    return pl.pallas_call(
        body,
        grid=(batch, nb),
        in_specs=[
            pl.BlockSpec((rows, HG_COLS), lambda b, r: (b * nb + r, 0)),
            pl.BlockSpec((DEPTH, REC_WIDTH), lambda b, r: (0, 0)),
            pl.BlockSpec((1, REC_WIDTH), lambda b, r: (0, 0)),
        ],
        out_specs=pl.BlockSpec((rows, REC_WIDTH), lambda b, r: (b * nb + r, 0)),
        out_shape=jax.ShapeDtypeStruct((n, REC_WIDTH), BF16),
        scratch_shapes=[pltpu.VMEM((REC_WIDTH, REC_WIDTH), F32)] + [rec] * 6,
        compiler_params=_cparams(("arbitrary", "arbitrary")),
        name="hgrn2",
    )(rest, lb_logits, nw)


def _dn_body(x_ref, a_ref, b_ref, cw_ref, alog_ref, dtb_ref, nw_ref, o_ref,
             carry_scr, xs_scr, s_scr, *, rows):
    nbatch = x_ref.shape[0]
    nchunk = rows // DN_CHUNK
    flat = nbatch * rows

    @pl.when(pl.program_id(0) == 0)
    def _():
        carry_scr[...] = jnp.zeros(carry_scr.shape, F32)
        s_scr[...] = jnp.zeros(s_scr.shape, F32)

    cw = cw_ref[...]
    ys = []
    for bi in range(nbatch):
        xs_scr[bi, 0:8, :] = carry_scr[bi]
        xs_scr[bi, 8:, :] = x_ref[bi, :, 0:DN_CONV_CH]
        carry_scr[bi] = x_ref[bi, rows - 8:rows, 0:DN_CONV_CH]
        y = cw[0:1] * xs_scr[bi, 5:5 + rows, :]
        for j in range(1, DN_CONV):
            y = y + cw[j:j + 1] * xs_scr[bi, 5 + j:5 + j + rows, :]
        ys.append(y)
    y = _silu(jnp.concatenate(ys, axis=0))
    z = x_ref[...].reshape(flat, DN_COLS)[:, DN_CONV_CH:]
    qc = y[:, 0:REC_WIDTH]
    kc = y[:, REC_WIDTH:2 * REC_WIDTH]
    vc = y[:, 2 * REC_WIDTH:]

    same = _same_head((REC_WIDTH, REC_WIDTH))
    ind = jnp.where(same, 1.0, 0.0).astype(BF16)
    qn = qc * lax.rsqrt(_dot((qc * qc).astype(BF16), ind) + EPS) * (HEAD_DIM ** -0.5)
    kn = kc * lax.rsqrt(_dot((kc * kc).astype(BF16), ind) + EPS)

    sp_in = a_ref[...].reshape(flat, REC_WIDTH) + dtb_ref[...]
    softplus = jnp.maximum(sp_in, 0.0) + jnp.log1p(jnp.exp(-jnp.abs(sp_in)))
    g = -jnp.exp(alog_ref[...]) * softplus
    beta = _sigmoid(b_ref[...].reshape(flat, REC_WIDTH))
    kb = kn * beta
    vb = vc * beta

    crow = lax.broadcasted_iota(jnp.int32, (DN_CHUNK, REC_WIDTH), 0)
    slane = lax.broadcasted_iota(jnp.int32, (DN_CHUNK, REC_WIDTH), 1) & (HEAD_DIM - 1)
    diag = crow == slane
    eye = jnp.where(diag, 1.0, 0.0).astype(F32)
    incl = _chunk_incl(DN_CHUNK, 6)

    def bdiag(v):
        v16 = v.astype(BF16)
        return jnp.where(same, jnp.concatenate([v16] * HEADS, axis=0), jnp.zeros((), BF16))

    chunks = [(bi, n) for n in range(nchunk) for bi in range(nbatch)]
    rows_of = {}
    for bi, n in chunks:
        start = bi * rows + n * DN_CHUNK
        rows_of[(bi, n)] = slice(start, start + DN_CHUNK)
    gc, g_row, kt_bd, a_kk, a_qk, xinv, pw, u, w = {}, {}, {}, {}, {}, {}, {}, {}, {}
    for ch in chunks:
        gc[ch] = _cumsum_rows(incl, g[rows_of[ch]])
    for ch in chunks:
        g_row[ch] = jnp.sum(jnp.where(diag, gc[ch], 0.0), axis=0, keepdims=True)
        kt = jnp.concatenate([kn[rows_of[ch]]] * HEADS, axis=0).T
        kt_bd[ch] = jnp.where(same, kt, 0.0)
    for ch in chunks:
        sl = rows_of[ch]
        decay = jnp.exp(jnp.where(slane <= crow, gc[ch] - g_row[ch], MASK_NEG))
        sc = _dot(jnp.concatenate([kb[sl], qn[sl]], axis=0).astype(BF16), kt_bd[ch].astype(BF16))
        a_kk[ch] = jnp.where(slane < crow, sc[:DN_CHUNK] * decay, 0.0)
        a_qk[ch] = sc[DN_CHUNK:] * decay
    for ch in chunks:
        xinv[ch] = eye - a_kk[ch]
        pw[ch] = _dot(a_kk[ch].astype(BF16), bdiag(a_kk[ch]))
    for _ in range(4):
        for ch in chunks:
            r = _dot(jnp.concatenate([xinv[ch], pw[ch]], axis=0).astype(BF16), bdiag(pw[ch]))
            xinv[ch] = xinv[ch] + r[:DN_CHUNK]
            pw[ch] = r[DN_CHUNK:]
    for ch in chunks:
        sl = rows_of[ch]
        xi = (xinv[ch] + _dot(xinv[ch].astype(BF16), bdiag(pw[ch]))).astype(BF16)
        u[ch] = _dot(xi, bdiag(vb[sl]))
        w[ch] = _dot(xi, bdiag(kb[sl] * jnp.exp(gc[ch])))
    outs = {}
    state = [s_scr[bi] for bi in range(nbatch)]
    for n in range(nchunk):
        for bi in range(nbatch):
            ch = (bi, n)
            q_dec = qn[rows_of[ch]] * jnp.exp(gc[ch])
            r1 = _dot(jnp.concatenate([w[ch], q_dec], axis=0).astype(BF16), state[bi].astype(BF16))
            v_new = u[ch] - r1[:DN_CHUNK]
            gl_row = gc[ch][DN_CHUNK - 1:DN_CHUNK]
            kdt_bd = kt_bd[ch] * jnp.exp(gl_row - g_row[ch])
            r2 = _dot(jnp.concatenate([a_qk[ch], kdt_bd], axis=0).astype(BF16), bdiag(v_new))
            outs[ch] = r1[DN_CHUNK:] + r2[:DN_CHUNK]
            state[bi] = state[bi] * jnp.exp(gl_row) + r2[DN_CHUNK:]
    for bi in range(nbatch):
        s_scr[bi] = state[bi]

    o = jnp.concatenate([outs[(bi, n)] for bi in range(nbatch) for n in range(nchunk)], axis=0)
    ms = _dot((o * o).astype(BF16), ind) * (1.0 / HEAD_DIM)
    o = o * lax.rsqrt(ms + EPS) * nw_ref[...] * _silu(z)
    o_ref[...] = o.reshape(nbatch, rows, REC_WIDTH).astype(BF16)


def _deltanet(rest, cw, alog, dtb, nw, batch, seq, rows=128):
    rest3 = rest.reshape(batch, seq, REST_COLS)
    body = functools.partial(_dn_body, rows=rows)
    ab0 = (HG_COLS + DN_COLS) // REC_WIDTH
    out = pl.pallas_call(
        body,
        grid=(seq // rows,),
        in_specs=[
            pl.BlockSpec((batch, rows, DN_COLS), lambda r: (0, r, HG_COLS // DN_COLS)),
            pl.BlockSpec((batch, rows, REC_WIDTH), lambda r: (0, r, ab0)),
            pl.BlockSpec((batch, rows, REC_WIDTH), lambda r: (0, r, ab0 + 1)),
            pl.BlockSpec((DN_CONV, DN_CONV_CH), lambda r: (0, 0)),
            pl.BlockSpec((1, REC_WIDTH), lambda r: (0, 0)),
            pl.BlockSpec((1, REC_WIDTH), lambda r: (0, 0)),
            pl.BlockSpec((1, REC_WIDTH), lambda r: (0, 0)),
        ],
        out_specs=pl.BlockSpec((batch, rows, REC_WIDTH), lambda r: (0, r, 0)),
        out_shape=jax.ShapeDtypeStruct((batch, seq, REC_WIDTH), BF16),
        scratch_shapes=[
            pltpu.VMEM((batch, 8, DN_CONV_CH), F32),
            pltpu.VMEM((batch, rows + 8, DN_CONV_CH), F32),
            pltpu.VMEM((batch, REC_WIDTH, REC_WIDTH), F32),
        ],
        compiler_params=_cparams(("arbitrary",)),
        name="gated_deltanet",
    )(rest3, rest3, rest3, cw, alog, dtb, nw)
    return out.reshape(batch * seq, REC_WIDTH)


def _mix_ffn_body(att_ref, hg_ref, dn_ref, h_ref, wo_ref, nw_ref, wup_ref, cw_ref, cb_ref, wdn_ref,
                  fw_ref, o_ref, u_scr, act_scr, *, tm, tf, blocks_per_seq, final):
    first = pl.program_id(0) % blocks_per_seq == 0

    @pl.when(first)
    def _():
        u_scr[:HALO, :] = jnp.zeros((HALO, D_MODEL), BF16)

    @pl.when(jnp.logical_not(first))
    def _():
        u_scr[:HALO, :] = u_scr[tm:tm + HALO, :]

    h1 = h_ref[...] + _dot(att_ref[...], wo_ref[0:ATT_WIDTH, :])
    h1 = h1 + _dot(hg_ref[...], wo_ref[ATT_WIDTH:ATT_WIDTH + REC_WIDTH, :])
    h1 = h1 + _dot(dn_ref[...], wo_ref[ATT_WIDTH + REC_WIDTH:, :])
    o_ref[...] = h1
    u_scr[HALO:, :] = _rms(h1, nw_ref[...]).astype(BF16)
    u = u_scr[...]

    def conv(zz, w, b):
        lo = HALO - (FFN_CONV - 1)
        out = b
        for j in range(FFN_CONV):
            out = out + w[j:j + 1] * zz[lo + j:lo + j + tm]
        return out

    for j in range(D_FF // tf):
        gs = slice(j * tf, (j + 1) * tf)
        vs = slice(D_FF + j * tf, D_FF + (j + 1) * tf)
        gate = conv(_dot(u, wup_ref[:, gs]), cw_ref[:, gs], cb_ref[:, gs])
        val = conv(_dot(u, wup_ref[:, vs]), cw_ref[:, vs], cb_ref[:, vs])
        act_scr[:, gs] = (_silu(gate) * val).astype(BF16)

    act = act_scr[...]
    nstep = 256
    for c0 in range(0, D_MODEL, nstep):
        y = o_ref[:, c0:c0 + nstep] + _dot(act, wdn_ref[:, c0:c0 + nstep])
        o_ref[:, c0:c0 + nstep] = y
    if final:
        o_ref[...] = _rms(o_ref[...], fw_ref[...])


def _mix_ffn(att_o, hg_o, dn_o, h, wo, nw, wup, cw, cb, wdn, fw, seq, final, tm=512, tf=256):
    n = h.shape[0]
    body = functools.partial(_mix_ffn_body, tm=tm, tf=tf, blocks_per_seq=seq // tm, final=final)
    return pl.pallas_call(
        body,
        grid=(n // tm,),
        in_specs=[
            pl.BlockSpec((tm, ATT_WIDTH), lambda i: (i, 0)),
            pl.BlockSpec((tm, REC_WIDTH), lambda i: (i, 0)),
            pl.BlockSpec((tm, REC_WIDTH), lambda i: (i, 0)),
            pl.BlockSpec((tm, D_MODEL), lambda i: (i, 0)),
            _const_spec((D_MODEL, D_MODEL)),
            _const_spec((1, D_MODEL)),
            _const_spec((D_MODEL, 2 * D_FF)),
            _const_spec((FFN_CONV, 2 * D_FF)),
            _const_spec((1, 2 * D_FF)),
            _const_spec((D_FF, D_MODEL)),
            _const_spec((1, D_MODEL)),
        ],
        out_specs=pl.BlockSpec((tm, D_MODEL), lambda i: (i, 0)),
        out_shape=jax.ShapeDtypeStruct((n, D_MODEL), F32),
        scratch_shapes=[
            pltpu.VMEM((HALO + tm, D_MODEL), BF16),
            pltpu.VMEM((tm, D_FF), BF16),
        ],
        compiler_params=_cparams(("arbitrary",)),
        name="mix_ffn",
    )(att_o, hg_o, dn_o, h, wo, nw, wup, cw, cb, wdn, fw)


def _prep_w_in(w):
    nmain = D_IN - 2 * HEADS
    q = w[:, :ATT_WIDTH] * (ATT_QK_DIM ** -0.5)
    ab = jnp.repeat(w[:, nmain:], HEAD_DIM, axis=-1)
    return jnp.concatenate([q.astype(BF16), w[:, ATT_WIDTH:nmain].astype(BF16), ab.astype(BF16)],
                           axis=-1)


def kernel(x, attn_norm_w, w_in, diff_lambda, diff_subln_w, hgrn_lb_logits, hgrn_norm_w, dn_conv_w,
           dn_A_log, dn_dt_bias, dn_norm_w, w_out, ffn_norm_w, ffn_w_up, ffn_conv_w, ffn_conv_b,
           ffn_w_down, final_norm_w):
    batch, seq, _ = x.shape
    h = x.reshape(batch * seq, D_MODEL)
    dn_alog = jnp.repeat(dn_A_log, HEAD_DIM, axis=-1)
    dn_dtb = jnp.repeat(dn_dt_bias, HEAD_DIM, axis=-1)
    hg_nw = jnp.tile(hgrn_norm_w, (1, HEADS))
    dn_nw = jnp.tile(dn_norm_w, (1, HEADS))
    for l in range(DEPTH):
        lambda_init = 0.8 - 0.6 * math.exp(-0.3 * l)
        att, rest = _inproj(h, attn_norm_w[l][None], _prep_w_in(w_in[l]))
        att_o = _attention(att, diff_lambda[l], diff_subln_w[l][None], batch, seq, lambda_init)
        hg_o = _hgrn(rest, hgrn_lb_logits, hg_nw[l][None], batch, seq, l)
        dn_o = _deltanet(rest, dn_conv_w[l], dn_alog[l][None], dn_dtb[l][None], dn_nw[l][None],
                         batch, seq)
        h = _mix_ffn(att_o, hg_o, dn_o, h, w_out[l].astype(BF16), ffn_norm_w[l][None],
                     ffn_w_up[l].astype(BF16), ffn_conv_w[l], ffn_conv_b[l][None],
                     ffn_w_down[l].astype(BF16), final_norm_w[None], seq, final=(l == DEPTH - 1))
    return h.reshape(batch, seq, D_MODEL)
```

```python
import functools
import math

import jax
import jax.numpy as jnp
from jax import lax
from jax.experimental import pallas as pl
from jax.experimental.pallas import tpu as pltpu

F32 = jnp.float32
BF16 = jnp.bfloat16
HIGHEST = lax.Precision.HIGHEST

D_MODEL = 1024
DEPTH = 2
ATT_QK_DIM = 64
ATT_V_DIM = 128
ATT_HEADS = 4
ATT_WIDTH = 512
HEADS = 4
HEAD_DIM = 64
REC_WIDTH = HEADS * HEAD_DIM
DN_CONV = 4
DN_CONV_CH = 3 * REC_WIDTH
FFN_CONV = 3
D_FF = 2816
HG_CHUNK = 16
DN_CHUNK = 64
EPS = 1e-6
MASK_NEG = -1e30
F_FLOOR = 1e-30
LOG2E = math.log2(math.e)

ATT_COLS = 3 * ATT_WIDTH
HG_COLS = 4 * REC_WIDTH
DN_COLS = 4 * REC_WIDTH
AB_COLS = 2 * REC_WIDTH
REST_COLS = HG_COLS + DN_COLS + AB_COLS
D_IN = ATT_COLS + HG_COLS + DN_COLS + 2 * HEADS
D_IN_WIDE = ATT_COLS + REST_COLS

VMEM_LIMIT = 56 * 1024 * 1024
LANES = 128
HALO = 16


def _cparams(sem):
    return pltpu.CompilerParams(dimension_semantics=sem, vmem_limit_bytes=VMEM_LIMIT)


def _const_spec(shape):
    return pl.BlockSpec(shape, lambda *_: (0,) * len(shape), pipeline_mode=pl.Buffered(1))


def _rms(x, w):
    return x * lax.rsqrt(jnp.mean(x * x, axis=-1, keepdims=True) + EPS) * w


def _sigmoid(x):
    return 1.0 / (1.0 + jnp.exp(-x))


def _silu(x):
    return x * _sigmoid(x)


def _dot(a, b, precision=None):
    return jnp.dot(a, b, preferred_element_type=F32, precision=precision)


def _head_of(idx):
    return lax.shift_right_logical(idx, 6)


def _same_head(shape):
    r = lax.broadcasted_iota(jnp.int32, shape, 0)
    c = lax.broadcasted_iota(jnp.int32, shape, 1)
    return _head_of(r) == _head_of(c)


def _cumsum_rows(incl, x):
    hi = x.astype(BF16)
    r1 = x - hi.astype(F32)
    mid = r1.astype(BF16)
    lo = (r1 - mid.astype(F32)).astype(BF16)
    return _dot(incl, hi) + _dot(incl, mid) + _dot(incl, lo)


def _chunk_incl(rows, chunk_shift):
    r = lax.broadcasted_iota(jnp.int32, (rows, rows), 0)
    c = lax.broadcasted_iota(jnp.int32, (rows, rows), 1)
    same = lax.shift_right_logical(r, chunk_shift) == lax.shift_right_logical(c, chunk_shift)
    return jnp.where(same & (c <= r), 1.0, 0.0).astype(BF16)


def _inproj_body(x_ref, nw_ref, w_ref, att_ref, rest_ref):
    u = _rms(x_ref[...], nw_ref[...]).astype(BF16)
    step = 512
    for c0 in range(0, ATT_COLS, step):
        att_ref[:, c0:c0 + step] = _dot(u, w_ref[:, c0:c0 + step]).astype(BF16)
    for c0 in range(0, REST_COLS, step):
        rest_ref[:, c0:c0 + step] = _dot(u, w_ref[:, ATT_COLS + c0:ATT_COLS + c0 + step])


def _inproj(h, nw, w, tm=512):
    n = h.shape[0]
    return pl.pallas_call(
        _inproj_body,
        grid=(n // tm,),
        in_specs=[
            pl.BlockSpec((tm, D_MODEL), lambda i: (i, 0)),
            _const_spec((1, D_MODEL)),
            _const_spec((D_MODEL, D_IN_WIDE)),
        ],
        out_specs=[
            pl.BlockSpec((tm, ATT_COLS), lambda i: (i, 0)),
            pl.BlockSpec((tm, REST_COLS), lambda i: (i, 0)),
        ],
        out_shape=[
            jax.ShapeDtypeStruct((n, ATT_COLS), BF16),
            jax.ShapeDtypeStruct((n, REST_COLS), F32),
        ],
        compiler_params=_cparams(("arbitrary",)),
        name="inproj",
    )(h, nw, w)


def _attn_body(lamp_ref, subw_ref, q_ref, k_ref, v_ref, o_ref,
               vt_scr, qt_scr, sa_scr, sb_scr, ma_scr, mb_scr, pa_scr, pb_scr, m_scr, l_scr, al_scr,
               acc_scr, *, tq, tk, lambda_init):
    i = pl.program_id(2)
    nblk = vt_scr.shape[0] - 1

    @pl.when(i == 0)
    def _():
        for c in range(nblk):
            vt_scr[c] = v_ref[c * tk:(c + 1) * tk, :].astype(F32).T.astype(BF16)
        vt_scr[nblk] = jnp.zeros(vt_scr.shape[1:], BF16)
        pb_scr[...] = jnp.zeros(pb_scr.shape, BF16)

    qt = q_ref[...].astype(F32).T * LOG2E
    first = lax.broadcasted_iota(jnp.int32, qt.shape, 0) < ATT_QK_DIM
    qt_scr[:, :tq] = jnp.where(first, qt, 0.0).astype(BF16)
    qt_scr[:, tq:] = jnp.where(first, 0.0, qt).astype(BF16)
    m_scr[...] = jnp.full(m_scr.shape, MASK_NEG, F32)
    l_scr[...] = jnp.zeros(l_scr.shape, F32)
    al_scr[...] = jnp.ones(al_scr.shape, F32)
    acc_scr[...] = jnp.zeros(acc_scr.shape, F32)

    def put_scores(dst, j):
        kj = k_ref[pl.ds(pl.multiple_of(j * tk, tk), tk), :]
        s = _dot(kj, qt_scr[...])
        dst[0][...] = s
        dst[1][...] = jnp.max(s, axis=0, keepdims=True)

    def phase(j, s_cur, p_cur, s_nxt, p_prv, masked, prefetch=True):
        if prefetch:
            put_scores(s_nxt, j + 1)
        pv = _dot(vt_scr[jnp.where(j == 0, nblk, j - 1)], p_prv[...])
        acc_scr[...] = al_scr[...] * acc_scr[...] + pv
        s = s_cur[0][...]
        if masked:
            kpos = lax.broadcasted_iota(jnp.int32, s.shape, 0) + (j * tk - i * tq)
            qpos = lax.broadcasted_iota(jnp.int32, s.shape, 1)
            qpos = jnp.where(qpos >= tq, qpos - tq, qpos)
            s = jnp.where(kpos <= qpos, s, MASK_NEG)
            m_tile = jnp.max(s, axis=0, keepdims=True)
        else:
            m_tile = s_cur[1][...]
        m_old = m_scr[...]
        m_new = jnp.maximum(m_old, m_tile)
        alpha = jnp.exp2(m_old - m_new)
        p = jnp.exp2(s - m_new)
        l_scr[...] = alpha * l_scr[...] + jnp.sum(p, axis=0, keepdims=True)
        m_scr[...] = m_new
        p_cur[...] = p.astype(BF16)
        al_scr[...] = alpha

    def finish(j, p_cur):
        acc_scr[...] = al_scr[...] * acc_scr[...] + _dot(vt_scr[j], p_cur[...])

    sa = (sa_scr, ma_scr)
    sb = (sb_scr, mb_scr)
    put_scores(sa, 0)

    def pair(jj, carry):
        phase(2 * jj, sa, pa_scr, sb, pb_scr, False)
        phase(2 * jj + 1, sb, pb_scr, sa, pa_scr, False)
        return carry

    lax.fori_loop(0, lax.shift_right_logical(i, 1), pair, 0)

    @pl.when((i & 1) == 0)
    def _():
        phase(i, sa, pa_scr, sb, pb_scr, True, prefetch=False)
        finish(i, pa_scr)

    @pl.when((i & 1) == 1)
    def _():
        phase(i - 1, sa, pa_scr, sb, pb_scr, False)
        phase(i, sb, pb_scr, sa, pa_scr, True, prefetch=False)
        finish(i, pb_scr)

    acc = acc_scr[...]
    lp = lamp_ref[...]
    lam = (jnp.exp(jnp.sum(lp[0:1] * lp[1:2], axis=-1, keepdims=True))
           - jnp.exp(jnp.sum(lp[2:3] * lp[3:4], axis=-1, keepdims=True)) + lambda_init)
    rl = 1.0 / l_scr[...]
    ot = acc[:, :tq] * rl[:, :tq] - lam * (acc[:, tq:] * rl[:, tq:])
    ot = ot * (lax.rsqrt(jnp.mean(ot * ot, axis=0, keepdims=True) + EPS) * (1.0 - lambda_init))
    o_ref[...] = (ot.T * subw_ref[...]).astype(BF16)


def _attention(att, lamp, subw, batch, seq, lambda_init, tq=512):
    n = att.shape[0]
    nq = seq // tq
    tk = tq
    body = functools.partial(_attn_body, tq=tq, tk=tk, lambda_init=lambda_init)
    return pl.pallas_call(
        body,
        grid=(batch, ATT_HEADS, nq),
        in_specs=[
            pl.BlockSpec((4, ATT_QK_DIM), lambda b, h, i: (0, 0)),
            pl.BlockSpec((1, ATT_V_DIM), lambda b, h, i: (0, 0)),
            pl.BlockSpec((tq, LANES), lambda b, h, i: (b * nq + i, h)),
            pl.BlockSpec((seq, LANES), lambda b, h, i: (b, ATT_HEADS + h)),
            pl.BlockSpec((seq, LANES), lambda b, h, i: (b, 2 * ATT_HEADS + h)),
        ],
        out_specs=pl.BlockSpec((tq, LANES), lambda b, h, i: (b * nq + i, h)),
        out_shape=jax.ShapeDtypeStruct((n, ATT_WIDTH), BF16),
        scratch_shapes=[
            pltpu.VMEM((seq // tk + 1, LANES, tk), BF16),
            pltpu.VMEM((LANES, 2 * tq), BF16),
            pltpu.VMEM((tk, 2 * tq), F32),
            pltpu.VMEM((tk, 2 * tq), F32),
            pltpu.VMEM((1, 2 * tq), F32),
            pltpu.VMEM((1, 2 * tq), F32),
            pltpu.VMEM((tk, 2 * tq), BF16),
            pltpu.VMEM((tk, 2 * tq), BF16),
            pltpu.VMEM((1, 2 * tq), F32),
            pltpu.VMEM((1, 2 * tq), F32),
            pltpu.VMEM((1, 2 * tq), F32),
            pltpu.VMEM((LANES, 2 * tq), F32),
        ],
        compiler_params=_cparams(("arbitrary", "arbitrary", "arbitrary")),
        name="diff_attention",
    )(lamp, subw, att, att, att)


def _hgrn_body(x_ref, lbl_ref, nw_ref, o_ref, st_scr, b_scr, q_scr, k_scr, v_scr,
               qd_scr, oi_scr, *, layer, rows):
    @pl.when(pl.program_id(1) == 0)
    def _():
        st_scr[...] = jnp.zeros(st_scr.shape, F32)

    lg = lbl_ref[...]
    e = jnp.exp(lg - jnp.max(lg, axis=0, keepdims=True))
    sm = e / jnp.sum(e, axis=0, keepdims=True)
    lb = jnp.zeros((1, REC_WIDTH), F32)
    for i in range(1, layer + 1):
        lb = lb + sm[i:i + 1]

    x = x_ref[...]
    q = x[:, 0:REC_WIDTH]
    fp = x[:, REC_WIDTH:2 * REC_WIDTH]
    iv = x[:, 2 * REC_WIDTH:3 * REC_WIDTH]
    gate = x[:, 3 * REC_WIDTH:4 * REC_WIDTH]
    qf = _silu(q)
    f = lb + (1.0 - lb) * _sigmoid(fp)
    logf = jnp.log2(jnp.maximum(f, F_FLOOR))
    kf = (1.0 - lb) * _sigmoid(-fp)

    b = _cumsum_rows(_chunk_incl(rows, 4), logf)
    b_scr[...] = b
    q_scr[...] = qf
    k_scr[...] = kf
    v_scr[...] = iv
    qd_scr[...] = qf * jnp.exp2(b)

    same = _same_head((REC_WIDTH, REC_WIDTH))
    ind = jnp.where(same, 1.0, 0.0).astype(BF16)
    trow = lax.broadcasted_iota(jnp.int32, (HG_CHUNK, REC_WIDTH), 0)

    def chunk(c, carry):
        base = pl.multiple_of(c * HG_CHUNK, HG_CHUNK)
        sl = pl.ds(base, HG_CHUNK)
        b_c = b_scr[sl, :]
        q_c = q_scr[sl, :]
        b_last = b_scr[pl.ds(base + HG_CHUNK - 1, 1), :]
        kd_c = k_scr[sl, :] * jnp.exp2(b_last - b_c)
        st = st_scr[...]
        o_inter = lax.dot_general(qd_scr[sl, :].astype(BF16), st.astype(BF16),
                                  (((1,), (1,)), ((), ())), preferred_element_type=F32)
        kvt = lax.dot_general(v_scr[sl, :].astype(BF16), kd_c.astype(BF16),
                              (((0,), (0,)), ((), ())), preferred_element_type=F32)
        st_scr[...] = st * jnp.exp2(b_last) + jnp.where(same, kvt, 0.0)
        slabs = []
        for s in range(HG_CHUNK):
            row = pl.ds(base + s, 1)
            rel = jnp.where(trow >= s, b_c - b_scr[row, :], MASK_NEG)
            slabs.append((jnp.exp2(rel) * q_c * k_scr[row, :]).astype(BF16))
        a = _dot(jnp.concatenate(slabs, axis=0), ind)
        o_intra = a[0:HG_CHUNK] * v_scr[pl.ds(base, 1), :]
        for s in range(1, HG_CHUNK):
            o_intra = o_intra + a[s * HG_CHUNK:(s + 1) * HG_CHUNK] * v_scr[pl.ds(base + s, 1), :]
        oi_scr[sl, :] = o_inter + o_intra
        return carry

    lax.fori_loop(0, rows // HG_CHUNK, chunk, 0, unroll=4)

    o = oi_scr[...]
    ms = _dot((o * o).astype(BF16), ind) * (1.0 / HEAD_DIM)
    y = o * lax.rsqrt(ms + EPS) * nw_ref[...] * _silu(gate)
    o_ref[...] = y.astype(BF16)


def _hgrn(rest, lb_logits, nw, batch, seq, layer, rows=256):
    n = rest.shape[0]
    nb = seq // rows
    body = functools.partial(_hgrn_body, layer=layer, rows=rows)
    rec = pltpu.VMEM((rows, REC_WIDTH), F32)
    return pl.pallas_call(
        body,
        grid=(batch, nb),
        in_specs=[
            pl.BlockSpec((rows, HG_COLS), lambda b, r: (b * nb + r, 0)),
            pl.BlockSpec((DEPTH, REC_WIDTH), lambda b, r: (0, 0)),
            pl.BlockSpec((1, REC_WIDTH), lambda b, r: (0, 0)),
        ],
        out_specs=pl.BlockSpec((rows, REC_WIDTH), lambda b, r: (b * nb + r, 0)),
        out_shape=jax.ShapeDtypeStruct((n, REC_WIDTH), BF16),
        scratch_shapes=[pltpu.VMEM((REC_WIDTH, REC_WIDTH), F32)] + [rec] * 6,
        compiler_params=_cparams(("arbitrary", "arbitrary")),
        name="hgrn2",
    )(rest, lb_logits, nw)


def _dn_body(x_ref, a_ref, b_ref, cw_ref, alog_ref, dtb_ref, nw_ref, o_ref,
             carry_scr, xs_scr, s_scr, *, rows):
    nbatch = x_ref.shape[0]
    nchunk = rows // DN_CHUNK
    flat = nbatch * rows

    @pl.when(pl.program_id(0) == 0)
    def _():
        carry_scr[...] = jnp.zeros(carry_scr.shape, F32)
        s_scr[...] = jnp.zeros(s_scr.shape, F32)

    cw = cw_ref[...]
    ys = []
    for bi in range(nbatch):
        xs_scr[bi, 0:8, :] = carry_scr[bi]
        xs_scr[bi, 8:, :] = x_ref[bi, :, 0:DN_CONV_CH]
        carry_scr[bi] = x_ref[bi, rows - 8:rows, 0:DN_CONV_CH]
        y = cw[0:1] * xs_scr[bi, 5:5 + rows, :]
        for j in range(1, DN_CONV):
            y = y + cw[j:j + 1] * xs_scr[bi, 5 + j:5 + j + rows, :]
        ys.append(y)
    y = _silu(jnp.concatenate(ys, axis=0))
    z = x_ref[...].reshape(flat, DN_COLS)[:, DN_CONV_CH:]
    qc = y[:, 0:REC_WIDTH]
    kc = y[:, REC_WIDTH:2 * REC_WIDTH]
    vc = y[:, 2 * REC_WIDTH:]

    same = _same_head((REC_WIDTH, REC_WIDTH))
    ind = jnp.where(same, 1.0, 0.0).astype(BF16)
    qn = qc * lax.rsqrt(_dot((qc * qc).astype(BF16), ind) + EPS) * (HEAD_DIM ** -0.5)
    kn = kc * lax.rsqrt(_dot((kc * kc).astype(BF16), ind) + EPS)

    sp_in = a_ref[...].reshape(flat, REC_WIDTH) + dtb_ref[...]
    softplus = jnp.maximum(sp_in, 0.0) + jnp.log1p(jnp.exp(-jnp.abs(sp_in)))
    g = -jnp.exp(alog_ref[...]) * softplus
    beta = _sigmoid(b_ref[...].reshape(flat, REC_WIDTH))
    kb = kn * beta
    vb = vc * beta

    crow = lax.broadcasted_iota(jnp.int32, (DN_CHUNK, REC_WIDTH), 0)
    slane = lax.broadcasted_iota(jnp.int32, (DN_CHUNK, REC_WIDTH), 1) & (HEAD_DIM - 1)
    diag = crow == slane
    eye = jnp.where(diag, 1.0, 0.0).astype(F32)
    incl = _chunk_incl(DN_CHUNK, 6)

    def bdiag(v):
        v16 = v.astype(BF16)
        return jnp.where(same, jnp.concatenate([v16] * HEADS, axis=0), jnp.zeros((), BF16))

    chunks = [(bi, n) for n in range(nchunk) for bi in range(nbatch)]
    rows_of = {}
    for bi, n in chunks:
        start = bi * rows + n * DN_CHUNK
        rows_of[(bi, n)] = slice(start, start + DN_CHUNK)
    gc, g_row, kt_bd, a_kk, a_qk, xinv, pw, u, w = {}, {}, {}, {}, {}, {}, {}, {}, {}
    for ch in chunks:
        gc[ch] = _cumsum_rows(incl, g[rows_of[ch]])
    for ch in chunks:
        g_row[ch] = jnp.sum(jnp.where(diag, gc[ch], 0.0), axis=0, keepdims=True)
        kt = jnp.concatenate([kn[rows_of[ch]]] * HEADS, axis=0).T
        kt_bd[ch] = jnp.where(same, kt, 0.0)
    for ch in chunks:
        sl = rows_of[ch]
        decay = jnp.exp(jnp.where(slane <= crow, gc[ch] - g_row[ch], MASK_NEG))
        sc = _dot(jnp.concatenate([kb[sl], qn[sl]], axis=0).astype(BF16), kt_bd[ch].astype(BF16))
        a_kk[ch] = jnp.where(slane < crow, sc[:DN_CHUNK] * decay, 0.0)
        a_qk[ch] = sc[DN_CHUNK:] * decay
    for ch in chunks:
        xinv[ch] = eye - a_kk[ch]
        pw[ch] = _dot(a_kk[ch].astype(BF16), bdiag(a_kk[ch]))
    for _ in range(4):
        for ch in chunks:
            r = _dot(jnp.concatenate([xinv[ch], pw[ch]], axis=0).astype(BF16), bdiag(pw[ch]))
            xinv[ch] = xinv[ch] + r[:DN_CHUNK]
            pw[ch] = r[DN_CHUNK:]
    for ch in chunks:
        sl = rows_of[ch]
        xi = (xinv[ch] + _dot(xinv[ch].astype(BF16), bdiag(pw[ch]))).astype(BF16)
        u[ch] = _dot(xi, bdiag(vb[sl]))
        w[ch] = _dot(xi, bdiag(kb[sl] * jnp.exp(gc[ch])))
    outs = {}
    state = [s_scr[bi] for bi in range(nbatch)]
    for n in range(nchunk):
        for bi in range(nbatch):
            ch = (bi, n)
            q_dec = qn[rows_of[ch]] * jnp.exp(gc[ch])
            r1 = _dot(jnp.concatenate([w[ch], q_dec], axis=0).astype(BF16), state[bi].astype(BF16))
            v_new = u[ch] - r1[:DN_CHUNK]
            gl_row = gc[ch][DN_CHUNK - 1:DN_CHUNK]
            kdt_bd = kt_bd[ch] * jnp.exp(gl_row - g_row[ch])
            r2 = _dot(jnp.concatenate([a_qk[ch], kdt_bd], axis=0).astype(BF16), bdiag(v_new))
            outs[ch] = r1[DN_CHUNK:] + r2[:DN_CHUNK]
            state[bi] = state[bi] * jnp.exp(gl_row) + r2[DN_CHUNK:]
    for bi in range(nbatch):
        s_scr[bi] = state[bi]

    o = jnp.concatenate([outs[(bi, n)] for bi in range(nbatch) for n in range(nchunk)], axis=0)
    ms = _dot((o * o).astype(BF16), ind) * (1.0 / HEAD_DIM)
    o = o * lax.rsqrt(ms + EPS) * nw_ref[...] * _silu(z)
    o_ref[...] = o.reshape(nbatch, rows, REC_WIDTH).astype(BF16)


def _deltanet(rest, cw, alog, dtb, nw, batch, seq, rows=128):
    rest3 = rest.reshape(batch, seq, REST_COLS)
    body = functools.partial(_dn_body, rows=rows)
    ab0 = (HG_COLS + DN_COLS) // REC_WIDTH
    out = pl.pallas_call(
        body,
        grid=(seq // rows,),
        in_specs=[
            pl.BlockSpec((batch, rows, DN_COLS), lambda r: (0, r, HG_COLS // DN_COLS)),
            pl.BlockSpec((batch, rows, REC_WIDTH), lambda r: (0, r, ab0)),
            pl.BlockSpec((batch, rows, REC_WIDTH), lambda r: (0, r, ab0 + 1)),
            pl.BlockSpec((DN_CONV, DN_CONV_CH), lambda r: (0, 0)),
            pl.BlockSpec((1, REC_WIDTH), lambda r: (0, 0)),
            pl.BlockSpec((1, REC_WIDTH), lambda r: (0, 0)),
            pl.BlockSpec((1, REC_WIDTH), lambda r: (0, 0)),
        ],
        out_specs=pl.BlockSpec((batch, rows, REC_WIDTH), lambda r: (0, r, 0)),
        out_shape=jax.ShapeDtypeStruct((batch, seq, REC_WIDTH), BF16),
        scratch_shapes=[
            pltpu.VMEM((batch, 8, DN_CONV_CH), F32),
            pltpu.VMEM((batch, rows + 8, DN_CONV_CH), F32),
            pltpu.VMEM((batch, REC_WIDTH, REC_WIDTH), F32),
        ],
        compiler_params=_cparams(("arbitrary",)),
        name="gated_deltanet",
    )(rest3, rest3, rest3, cw, alog, dtb, nw)
    return out.reshape(batch * seq, REC_WIDTH)


def _mix_ffn_body(att_ref, hg_ref, dn_ref, h_ref, wo_ref, nw_ref, wup_ref, cw_ref, cb_ref, wdn_ref,
                  fw_ref, o_ref, u_scr, act_scr, *, tm, tf, blocks_per_seq, final):
    first = pl.program_id(0) % blocks_per_seq == 0

    @pl.when(first)
    def _():
        u_scr[:HALO, :] = jnp.zeros((HALO, D_MODEL), BF16)

    @pl.when(jnp.logical_not(first))
    def _():
        u_scr[:HALO, :] = u_scr[tm:tm + HALO, :]

    h1 = h_ref[...] + _dot(att_ref[...], wo_ref[0:ATT_WIDTH, :])
    h1 = h1 + _dot(hg_ref[...], wo_ref[ATT_WIDTH:ATT_WIDTH + REC_WIDTH, :])
    h1 = h1 + _dot(dn_ref[...], wo_ref[ATT_WIDTH + REC_WIDTH:, :])
    o_ref[...] = h1
    u_scr[HALO:, :] = _rms(h1, nw_ref[...]).astype(BF16)
    u = u_scr[...]

    def conv(zz, w, b):
        lo = HALO - (FFN_CONV - 1)
        out = b
        for j in range(FFN_CONV):
            out = out + w[j:j + 1] * zz[lo + j:lo + j + tm]
        return out

    for j in range(D_FF // tf):
        gs = slice(j * tf, (j + 1) * tf)
        vs = slice(D_FF + j * tf, D_FF + (j + 1) * tf)
        gate = conv(_dot(u, wup_ref[:, gs]), cw_ref[:, gs], cb_ref[:, gs])
        val = conv(_dot(u, wup_ref[:, vs]), cw_ref[:, vs], cb_ref[:, vs])
        act_scr[:, gs] = (_silu(gate) * val).astype(BF16)

    act = act_scr[...]
    nstep = 256
    for c0 in range(0, D_MODEL, nstep):
        y = o_ref[:, c0:c0 + nstep] + _dot(act, wdn_ref[:, c0:c0 + nstep])
        o_ref[:, c0:c0 + nstep] = y
    if final:
        o_ref[...] = _rms(o_ref[...], fw_ref[...])


def _mix_ffn(att_o, hg_o, dn_o, h, wo, nw, wup, cw, cb, wdn, fw, seq, final, tm=512, tf=256):
    n = h.shape[0]
    body = functools.partial(_mix_ffn_body, tm=tm, tf=tf, blocks_per_seq=seq // tm, final=final)
    return pl.pallas_call(
        body,
        grid=(n // tm,),
        in_specs=[
            pl.BlockSpec((tm, ATT_WIDTH), lambda i: (i, 0)),
            pl.BlockSpec((tm, REC_WIDTH), lambda i: (i, 0)),
            pl.BlockSpec((tm, REC_WIDTH), lambda i: (i, 0)),
            pl.BlockSpec((tm, D_MODEL), lambda i: (i, 0)),
            _const_spec((D_MODEL, D_MODEL)),
            _const_spec((1, D_MODEL)),
            _const_spec((D_MODEL, 2 * D_FF)),
            _const_spec((FFN_CONV, 2 * D_FF)),
            _const_spec((1, 2 * D_FF)),
            _const_spec((D_FF, D_MODEL)),
            _const_spec((1, D_MODEL)),
        ],
        out_specs=pl.BlockSpec((tm, D_MODEL), lambda i: (i, 0)),
        out_shape=jax.ShapeDtypeStruct((n, D_MODEL), F32),
        scratch_shapes=[
            pltpu.VMEM((HALO + tm, D_MODEL), BF16),
            pltpu.VMEM((tm, D_FF), BF16),
        ],
        compiler_params=_cparams(("arbitrary",)),
        name="mix_ffn",
    )(att_o, hg_o, dn_o, h, wo, nw, wup, cw, cb, wdn, fw)


def _cast_body(x_ref, o_ref):
    o_ref[...] = x_ref[...].astype(BF16)


def _cast_layer(w, layer, rb=256):
    _, r, c = w.shape
    return pl.pallas_call(
        _cast_body,
        grid=(r // rb,),
        in_specs=[pl.BlockSpec((None, rb, c), lambda i: (layer, i, 0))],
        out_specs=pl.BlockSpec((rb, c), lambda i: (i, 0)),
        out_shape=jax.ShapeDtypeStruct((r, c), BF16),
        compiler_params=_cparams(("arbitrary",)),
        name="cast_weight",
    )(w)


W_IN_STEP = 512


def _prep_w_in_body(x_ref, o_ref):
    j = pl.program_id(0)
    last = D_IN_WIDE // W_IN_STEP - 1

    @pl.when(j == 0)
    def _():
        o_ref[...] = (x_ref[...] * (ATT_QK_DIM ** -0.5)).T.astype(BF16)

    @pl.when(jnp.logical_and(j > 0, j < last))
    def _():
        o_ref[...] = x_ref[...].T.astype(BF16)

    @pl.when(j == last)
    def _():
        rep = [jnp.broadcast_to(x_ref[r:r + 1, :], (HEAD_DIM, D_MODEL)) for r in range(2 * HEADS)]
        o_ref[...] = jnp.concatenate(rep, axis=0).T.astype(BF16)


def _prep_w_in(w_in, layer):
    assert D_IN - 2 * HEADS == D_IN_WIDE - W_IN_STEP and AB_COLS == W_IN_STEP
    w_t = jnp.swapaxes(w_in, 1, 2)
    return pl.pallas_call(
        _prep_w_in_body,
        grid=(D_IN_WIDE // W_IN_STEP,),
        in_specs=[pl.BlockSpec((None, W_IN_STEP, D_MODEL), lambda j: (layer, j, 0))],
        out_specs=pl.BlockSpec((D_MODEL, W_IN_STEP), lambda j: (0, j)),
        out_shape=jax.ShapeDtypeStruct((D_MODEL, D_IN_WIDE), BF16),
        compiler_params=_cparams(("arbitrary",)),
        name="prep_w_in",
    )(w_t)


def kernel(x, attn_norm_w, w_in, diff_lambda, diff_subln_w, hgrn_lb_logits, hgrn_norm_w, dn_conv_w,
           dn_A_log, dn_dt_bias, dn_norm_w, w_out, ffn_norm_w, ffn_w_up, ffn_conv_w, ffn_conv_b,
           ffn_w_down, final_norm_w):
    batch, seq, _ = x.shape
    h = x.reshape(batch * seq, D_MODEL)
    dn_alog = jnp.repeat(dn_A_log, HEAD_DIM, axis=-1)
    dn_dtb = jnp.repeat(dn_dt_bias, HEAD_DIM, axis=-1)
    hg_nw = jnp.tile(hgrn_norm_w, (1, HEADS))
    dn_nw = jnp.tile(dn_norm_w, (1, HEADS))
    for l in range(DEPTH):
        lambda_init = 0.8 - 0.6 * math.exp(-0.3 * l)
        att, rest = _inproj(h, attn_norm_w[l][None], _prep_w_in(w_in, l))
        att_o = _attention(att, diff_lambda[l], diff_subln_w[l][None], batch, seq, lambda_init)
        hg_o = _hgrn(rest, hgrn_lb_logits, hg_nw[l][None], batch, seq, l)
        dn_o = _deltanet(rest, dn_conv_w[l], dn_alog[l][None], dn_dtb[l][None], dn_nw[l][None],
                         batch, seq)
        h = _mix_ffn(att_o, hg_o, dn_o, h, _cast_layer(w_out, l), ffn_norm_w[l][None],
                     _cast_layer(ffn_w_up, l), ffn_conv_w[l], ffn_conv_b[l][None],
                     _cast_layer(ffn_w_down, l), final_norm_w[None], seq, final=(l == DEPTH - 1))
    return h.reshape(batch, seq, D_MODEL)
```

```python
import functools
import math

import jax
import jax.numpy as jnp
from jax import lax
from jax.experimental import pallas as pl
from jax.experimental.pallas import tpu as pltpu

F32 = jnp.float32
BF16 = jnp.bfloat16
HIGHEST = lax.Precision.HIGHEST

D_MODEL = 1024
DEPTH = 2
ATT_QK_DIM = 64
ATT_V_DIM = 128
ATT_HEADS = 4
ATT_WIDTH = 512
HEADS = 4
HEAD_DIM = 64
REC_WIDTH = HEADS * HEAD_DIM
DN_CONV = 4
DN_CONV_CH = 3 * REC_WIDTH
FFN_CONV = 3
D_FF = 2816
HG_CHUNK = 16
DN_CHUNK = 64
EPS = 1e-6
MASK_NEG = -1e30
F_FLOOR = 1e-30
LOG2E = math.log2(math.e)

ATT_COLS = 3 * ATT_WIDTH
HG_COLS = 4 * REC_WIDTH
DN_COLS = 4 * REC_WIDTH
AB_COLS = 2 * REC_WIDTH
REST_COLS = HG_COLS + DN_COLS + AB_COLS
D_IN = ATT_COLS + HG_COLS + DN_COLS + 2 * HEADS
D_IN_WIDE = ATT_COLS + REST_COLS

VMEM_LIMIT = 56 * 1024 * 1024
LANES = 128
HALO = 16

def _cparams(sem):
    return pltpu.CompilerParams(dimension_semantics=sem, vmem_limit_bytes=VMEM_LIMIT)


def _const_spec(shape):
    return pl.BlockSpec(shape, lambda *_: (0,) * len(shape), pipeline_mode=pl.Buffered(1))


def _rms(x, w):
    return x * lax.rsqrt(jnp.mean(x * x, axis=-1, keepdims=True) + EPS) * w


def _sigmoid(x):
    return 1.0 / (1.0 + jnp.exp(-x))


def _silu(x):
    return x * _sigmoid(x)


def _dot(a, b, precision=None):
    return jnp.dot(a, b, preferred_element_type=F32, precision=precision)


def _head_of(idx):
    return lax.shift_right_logical(idx, 6)


def _same_head(shape):
    r = lax.broadcasted_iota(jnp.int32, shape, 0)
    c = lax.broadcasted_iota(jnp.int32, shape, 1)
    return _head_of(r) == _head_of(c)


def _cumsum_rows(incl, x):
    hi = x.astype(BF16)
    r1 = x - hi.astype(F32)
    mid = r1.astype(BF16)
    lo = (r1 - mid.astype(F32)).astype(BF16)
    return _dot(incl, hi) + _dot(incl, mid) + _dot(incl, lo)


def _chunk_incl(rows, chunk_shift):
    r = lax.broadcasted_iota(jnp.int32, (rows, rows), 0)
    c = lax.broadcasted_iota(jnp.int32, (rows, rows), 1)
    same = lax.shift_right_logical(r, chunk_shift) == lax.shift_right_logical(c, chunk_shift)
    return jnp.where(same & (c <= r), 1.0, 0.0).astype(BF16)


def _inproj_body(x_ref, nw_ref, w_ref, att_ref, rest_ref):
    u = _rms(x_ref[...], nw_ref[...]).astype(BF16)
    step = 512
    for c0 in range(0, ATT_COLS, step):
        att_ref[:, c0:c0 + step] = _dot(u, w_ref[:, c0:c0 + step]).astype(BF16)
    for c0 in range(0, REST_COLS, step):
        rest_ref[:, c0:c0 + step] = _dot(u, w_ref[:, ATT_COLS + c0:ATT_COLS + c0 + step])


def _inproj(h, nw, w, tm=512):
    n = h.shape[0]
    return pl.pallas_call(
        _inproj_body,
        grid=(n // tm,),
        in_specs=[
            pl.BlockSpec((tm, D_MODEL), lambda i: (i, 0)),
            _const_spec((1, D_MODEL)),
            _const_spec((D_MODEL, D_IN_WIDE)),
        ],
        out_specs=[
            pl.BlockSpec((tm, ATT_COLS), lambda i: (i, 0)),
            pl.BlockSpec((tm, REST_COLS), lambda i: (i, 0)),
        ],
        out_shape=[
            jax.ShapeDtypeStruct((n, ATT_COLS), BF16),
            jax.ShapeDtypeStruct((n, REST_COLS), F32),
        ],
        compiler_params=_cparams(("arbitrary",)),
        name="inproj",
    )(h, nw, w)


def _attn_body(lamp_ref, subw_ref, q_ref, k_ref, v_ref, o_ref,
               vt_scr, qt_scr, sa_scr, sb_scr, ma_scr, mb_scr, pa_scr, pb_scr, m_scr, l_scr, al_scr,
               acc_scr, *, tq, tk, lambda_init):
    i = pl.program_id(2)
    nblk = vt_scr.shape[0] - 1

    @pl.when(i == 0)
    def _():
        for c in range(nblk):
            vt_scr[c] = v_ref[c * tk:(c + 1) * tk, :].astype(F32).T.astype(BF16)
        vt_scr[nblk] = jnp.zeros(vt_scr.shape[1:], BF16)
        pb_scr[...] = jnp.zeros(pb_scr.shape, BF16)

    qt = q_ref[...].astype(F32).T * LOG2E
    first = lax.broadcasted_iota(jnp.int32, qt.shape, 0) < ATT_QK_DIM
    qt_scr[:, :tq] = jnp.where(first, qt, 0.0).astype(BF16)
    qt_scr[:, tq:] = jnp.where(first, 0.0, qt).astype(BF16)
    m_scr[...] = jnp.full(m_scr.shape, MASK_NEG, F32)
    l_scr[...] = jnp.zeros(l_scr.shape, F32)
    al_scr[...] = jnp.ones(al_scr.shape, F32)
    acc_scr[...] = jnp.zeros(acc_scr.shape, F32)

    def put_scores(dst, j):
        kj = k_ref[pl.ds(pl.multiple_of(j * tk, tk), tk), :]
        s = _dot(kj, qt_scr[...])
        dst[0][...] = s
        dst[1][...] = jnp.max(s, axis=0, keepdims=True)

    def phase(j, s_cur, p_cur, s_nxt, p_prv, masked, prefetch=True):
        if prefetch:
            put_scores(s_nxt, j + 1)
        pv = _dot(vt_scr[jnp.where(j == 0, nblk, j - 1)], p_prv[...])
        acc_scr[...] = al_scr[...] * acc_scr[...] + pv
        s = s_cur[0][...]
        if masked:
            kpos = lax.broadcasted_iota(jnp.int32, s.shape, 0) + (j * tk - i * tq)
            qpos = lax.broadcasted_iota(jnp.int32, s.shape, 1)
            qpos = jnp.where(qpos >= tq, qpos - tq, qpos)
            s = jnp.where(kpos <= qpos, s, MASK_NEG)
            m_tile = jnp.max(s, axis=0, keepdims=True)
        else:
            m_tile = s_cur[1][...]
        m_old = m_scr[...]
        m_new = jnp.maximum(m_old, m_tile)
        alpha = jnp.exp2(m_old - m_new)
        p = jnp.exp2(s - m_new)
        l_scr[...] = alpha * l_scr[...] + jnp.sum(p, axis=0, keepdims=True)
        m_scr[...] = m_new
        p_cur[...] = p.astype(BF16)
        al_scr[...] = alpha

    def finish(j, p_cur):
        acc_scr[...] = al_scr[...] * acc_scr[...] + _dot(vt_scr[j], p_cur[...])

    sa = (sa_scr, ma_scr)
    sb = (sb_scr, mb_scr)
    put_scores(sa, 0)

    def pair(jj, carry):
        phase(2 * jj, sa, pa_scr, sb, pb_scr, False)
        phase(2 * jj + 1, sb, pb_scr, sa, pa_scr, False)
        return carry

    lax.fori_loop(0, lax.shift_right_logical(i, 1), pair, 0)

    @pl.when((i & 1) == 0)
    def _():
        phase(i, sa, pa_scr, sb, pb_scr, True, prefetch=False)
        finish(i, pa_scr)

    @pl.when((i & 1) == 1)
    def _():
        phase(i - 1, sa, pa_scr, sb, pb_scr, False)
        phase(i, sb, pb_scr, sa, pa_scr, True, prefetch=False)
        finish(i, pb_scr)

    acc = acc_scr[...]
    lp = lamp_ref[...]
    lam = (jnp.exp(jnp.sum(lp[0:1] * lp[1:2], axis=-1, keepdims=True))
           - jnp.exp(jnp.sum(lp[2:3] * lp[3:4], axis=-1, keepdims=True)) + lambda_init)
    rl = 1.0 / l_scr[...]
    ot = acc[:, :tq] * rl[:, :tq] - lam * (acc[:, tq:] * rl[:, tq:])
    ot = ot * (lax.rsqrt(jnp.mean(ot * ot, axis=0, keepdims=True) + EPS) * (1.0 - lambda_init))
    o_ref[...] = (ot.T * subw_ref[...]).astype(BF16)


def _attention(att, lamp, subw, batch, seq, lambda_init, tq=512):
    n = att.shape[0]
    nq = seq // tq
    tk = tq
    body = functools.partial(_attn_body, tq=tq, tk=tk, lambda_init=lambda_init)
    return pl.pallas_call(
        body,
        grid=(batch, ATT_HEADS, nq),
        in_specs=[
            pl.BlockSpec((4, ATT_QK_DIM), lambda b, h, i: (0, 0)),
            pl.BlockSpec((1, ATT_V_DIM), lambda b, h, i: (0, 0)),
            pl.BlockSpec((tq, LANES), lambda b, h, i: (b * nq + i, h)),
            pl.BlockSpec((seq, LANES), lambda b, h, i: (b, ATT_HEADS + h)),
            pl.BlockSpec((seq, LANES), lambda b, h, i: (b, 2 * ATT_HEADS + h)),
        ],
        out_specs=pl.BlockSpec((tq, LANES), lambda b, h, i: (b * nq + i, h)),
        out_shape=jax.ShapeDtypeStruct((n, ATT_WIDTH), BF16),
        scratch_shapes=[
            pltpu.VMEM((seq // tk + 1, LANES, tk), BF16),
            pltpu.VMEM((LANES, 2 * tq), BF16),
            pltpu.VMEM((tk, 2 * tq), F32),
            pltpu.VMEM((tk, 2 * tq), F32),
            pltpu.VMEM((1, 2 * tq), F32),
            pltpu.VMEM((1, 2 * tq), F32),
            pltpu.VMEM((tk, 2 * tq), BF16),
            pltpu.VMEM((tk, 2 * tq), BF16),
            pltpu.VMEM((1, 2 * tq), F32),
            pltpu.VMEM((1, 2 * tq), F32),
            pltpu.VMEM((1, 2 * tq), F32),
            pltpu.VMEM((LANES, 2 * tq), F32),
        ],
        compiler_params=_cparams(("arbitrary", "arbitrary", "arbitrary")),
        name="diff_attention",
    )(lamp, subw, att, att, att)


def _hgrn_body(x_ref, lbl_ref, nw_ref, o_ref, st_scr, b_scr, q_scr, k_scr, v_scr,
               qd_scr, oi_scr, *, layer, rows):
    @pl.when(pl.program_id(1) == 0)
    def _():
        st_scr[...] = jnp.zeros(st_scr.shape, F32)

    lg = lbl_ref[...]
    e = jnp.exp(lg - jnp.max(lg, axis=0, keepdims=True))
    sm = e / jnp.sum(e, axis=0, keepdims=True)
    lb = jnp.zeros((1, REC_WIDTH), F32)
    for i in range(1, layer + 1):
        lb = lb + sm[i:i + 1]

    x = x_ref[...]
    q = x[:, 0:REC_WIDTH]
    fp = x[:, REC_WIDTH:2 * REC_WIDTH]
    iv = x[:, 2 * REC_WIDTH:3 * REC_WIDTH]
    gate = x[:, 3 * REC_WIDTH:4 * REC_WIDTH]
    qf = _silu(q)
    f = lb + (1.0 - lb) * _sigmoid(fp)
    logf = jnp.log2(jnp.maximum(f, F_FLOOR))
    kf = (1.0 - lb) * _sigmoid(-fp)

    b = _cumsum_rows(_chunk_incl(rows, 4), logf)
    def put(ref, val):
        ref[0] = val[:, :LANES]
        ref[1] = val[:, LANES:]

    def rows_at(ref, sl):
        return jnp.concatenate([ref[0, sl, :], ref[1, sl, :]], axis=1)

    def row_rep(ref, r):
        return rows_at(ref, pl.ds(r, HG_CHUNK, stride=0))

    put(b_scr, b)
    put(k_scr, kf)
    put(v_scr, iv)
    q_scr[...] = qf
    qd_scr[...] = qf * jnp.exp2(b)

    same = _same_head((REC_WIDTH, REC_WIDTH))
    ind = jnp.where(same, 1.0, 0.0).astype(BF16)
    trow = lax.broadcasted_iota(jnp.int32, (HG_CHUNK, REC_WIDTH), 0)

    def chunk(c, carry):
        base = c * HG_CHUNK
        sl = pl.ds(base, HG_CHUNK)
        b_c = rows_at(b_scr, sl)
        q_c = q_scr[sl, :]
        b_last = row_rep(b_scr, base + HG_CHUNK - 1)
        kd_c = rows_at(k_scr, sl) * jnp.exp2(b_last - b_c)
        st = st_scr[...]
        o_inter = lax.dot_general(qd_scr[sl, :].astype(BF16), st.astype(BF16),
                                  (((1,), (1,)), ((), ())), preferred_element_type=F32)
        kvt = lax.dot_general(rows_at(v_scr, sl).astype(BF16), kd_c.astype(BF16),
                              (((0,), (0,)), ((), ())), preferred_element_type=F32)
        st_scr[...] = st * jnp.exp2(b_last[0:1]) + jnp.where(same, kvt, 0.0)
        slabs = []
        for s in range(HG_CHUNK):
            rel = jnp.where(trow >= s, b_c - row_rep(b_scr, base + s), MASK_NEG)
            slabs.append((jnp.exp2(rel) * q_c * row_rep(k_scr, base + s)).astype(BF16))
        a = _dot(jnp.concatenate(slabs, axis=0), ind)
        o_intra = a[0:HG_CHUNK] * row_rep(v_scr, base)
        for s in range(1, HG_CHUNK):
            o_intra = o_intra + a[s * HG_CHUNK:(s + 1) * HG_CHUNK] * row_rep(v_scr, base + s)
        oi_scr[sl, :] = o_inter + o_intra
        return carry

    for c in range(rows // HG_CHUNK):
        chunk(c, 0)

    o = oi_scr[...]
    ms = _dot((o * o).astype(BF16), ind) * (1.0 / HEAD_DIM)
    y = o * lax.rsqrt(ms + EPS) * nw_ref[...] * _silu(gate)
    o_ref[...] = y.astype(BF16)


def _hgrn(rest, lb_logits, nw, batch, seq, layer, rows=256):
    n = rest.shape[0]
    nb = seq // rows
    body = functools.partial(_hgrn_body, layer=layer, rows=rows)
    rec = pltpu.VMEM((rows, REC_WIDTH), F32)
    slab = pltpu.VMEM((REC_WIDTH // LANES, rows, LANES), F32)
    return pl.pallas_call(
        body,
        grid=(batch, nb),
        in_specs=[
            pl.BlockSpec((rows, HG_COLS), lambda b, r: (b * nb + r, 0)),
            pl.BlockSpec((DEPTH, REC_WIDTH), lambda b, r: (0, 0)),
            pl.BlockSpec((1, REC_WIDTH), lambda b, r: (0, 0)),
        ],
        out_specs=pl.BlockSpec((rows, REC_WIDTH), lambda b, r: (b * nb + r, 0)),
        out_shape=jax.ShapeDtypeStruct((n, REC_WIDTH), BF16),
        scratch_shapes=[pltpu.VMEM((REC_WIDTH, REC_WIDTH), F32),
                        slab, rec, slab, slab, rec, rec],
        compiler_params=_cparams(("arbitrary", "arbitrary")),
        name="hgrn2",
    )(rest, lb_logits, nw)


def _dn_body(x_ref, a_ref, b_ref, cw_ref, alog_ref, dtb_ref, nw_ref, o_ref,
             carry_scr, xs_scr, s_scr, *, rows):
    nbatch = x_ref.shape[0]
    nchunk = rows // DN_CHUNK
    flat = nbatch * rows

    @pl.when(pl.program_id(0) == 0)
    def _():
        carry_scr[...] = jnp.zeros(carry_scr.shape, F32)
        s_scr[...] = jnp.zeros(s_scr.shape, F32)

    cw = cw_ref[...]
    ys = []
    for bi in range(nbatch):
        xs_scr[bi, 0:8, :] = carry_scr[bi]
        xs_scr[bi, 8:, :] = x_ref[bi, :, 0:DN_CONV_CH]
        carry_scr[bi] = x_ref[bi, rows - 8:rows, 0:DN_CONV_CH]
        y = cw[0:1] * xs_scr[bi, 5:5 + rows, :]
        for j in range(1, DN_CONV):
            y = y + cw[j:j + 1] * xs_scr[bi, 5 + j:5 + j + rows, :]
        ys.append(y)
    y = _silu(jnp.concatenate(ys, axis=0))
    z = x_ref[...].reshape(flat, DN_COLS)[:, DN_CONV_CH:]
    qc = y[:, 0:REC_WIDTH]
    kc = y[:, REC_WIDTH:2 * REC_WIDTH]
    vc = y[:, 2 * REC_WIDTH:]

    same = _same_head((REC_WIDTH, REC_WIDTH))
    ind = jnp.where(same, 1.0, 0.0).astype(BF16)
    qn = qc * lax.rsqrt(_dot((qc * qc).astype(BF16), ind) + EPS) * (HEAD_DIM ** -0.5)
    kn = kc * lax.rsqrt(_dot((kc * kc).astype(BF16), ind) + EPS)

    sp_in = a_ref[...].reshape(flat, REC_WIDTH) + dtb_ref[...]
    softplus = jnp.maximum(sp_in, 0.0) + jnp.log1p(jnp.exp(-jnp.abs(sp_in)))
    g = -jnp.exp(alog_ref[...]) * softplus
    beta = _sigmoid(b_ref[...].reshape(flat, REC_WIDTH))
    kb = kn * beta
    vb = vc * beta

    crow = lax.broadcasted_iota(jnp.int32, (DN_CHUNK, REC_WIDTH), 0)
    slane = lax.broadcasted_iota(jnp.int32, (DN_CHUNK, REC_WIDTH), 1) & (HEAD_DIM - 1)
    diag = crow == slane
    eye = jnp.where(diag, 1.0, 0.0).astype(F32)
    incl = _chunk_incl(DN_CHUNK, 6)

    def bdiag(v):
        v16 = v.astype(BF16)
        return jnp.where(same, jnp.concatenate([v16] * HEADS, axis=0), jnp.zeros((), BF16))

    chunks = [(bi, n) for n in range(nchunk) for bi in range(nbatch)]
    rows_of = {}
    for bi, n in chunks:
        start = bi * rows + n * DN_CHUNK
        rows_of[(bi, n)] = slice(start, start + DN_CHUNK)
    gc, g_row, kt_bd, a_kk, a_qk, xinv, pw, u, w = {}, {}, {}, {}, {}, {}, {}, {}, {}
    for ch in chunks:
        gc[ch] = _cumsum_rows(incl, g[rows_of[ch]])
    for ch in chunks:
        g_row[ch] = jnp.sum(jnp.where(diag, gc[ch], 0.0), axis=0, keepdims=True)
        kt = jnp.concatenate([kn[rows_of[ch]]] * HEADS, axis=0).T
        kt_bd[ch] = jnp.where(same, kt, 0.0)
    for ch in chunks:
        sl = rows_of[ch]
        decay = jnp.exp(jnp.where(slane <= crow, gc[ch] - g_row[ch], MASK_NEG))
        sc = _dot(jnp.concatenate([kb[sl], qn[sl]], axis=0).astype(BF16), kt_bd[ch].astype(BF16))
        a_kk[ch] = jnp.where(slane < crow, sc[:DN_CHUNK] * decay, 0.0)
        a_qk[ch] = sc[DN_CHUNK:] * decay
    for ch in chunks:
        xinv[ch] = eye - a_kk[ch]
        pw[ch] = _dot(a_kk[ch].astype(BF16), bdiag(a_kk[ch]))
    for _ in range(4):
        for ch in chunks:
            r = _dot(jnp.concatenate([xinv[ch], pw[ch]], axis=0).astype(BF16), bdiag(pw[ch]))
            xinv[ch] = xinv[ch] + r[:DN_CHUNK]
            pw[ch] = r[DN_CHUNK:]
    for ch in chunks:
        sl = rows_of[ch]
        xi = (xinv[ch] + _dot(xinv[ch].astype(BF16), bdiag(pw[ch]))).astype(BF16)
        u[ch] = _dot(xi, bdiag(vb[sl]))
        w[ch] = _dot(xi, bdiag(kb[sl] * jnp.exp(gc[ch])))
    outs = {}
    state = [s_scr[bi] for bi in range(nbatch)]
    for n in range(nchunk):
        for bi in range(nbatch):
            ch = (bi, n)
            q_dec = qn[rows_of[ch]] * jnp.exp(gc[ch])
            r1 = _dot(jnp.concatenate([w[ch], q_dec], axis=0).astype(BF16), state[bi].astype(BF16))
            v_new = u[ch] - r1[:DN_CHUNK]
            gl_row = gc[ch][DN_CHUNK - 1:DN_CHUNK]
            kdt_bd = kt_bd[ch] * jnp.exp(gl_row - g_row[ch])
            r2 = _dot(jnp.concatenate([a_qk[ch], kdt_bd], axis=0).astype(BF16), bdiag(v_new))
            outs[ch] = r1[DN_CHUNK:] + r2[:DN_CHUNK]
            state[bi] = state[bi] * jnp.exp(gl_row) + r2[DN_CHUNK:]
    for bi in range(nbatch):
        s_scr[bi] = state[bi]

    o = jnp.concatenate([outs[(bi, n)] for bi in range(nbatch) for n in range(nchunk)], axis=0)
    ms = _dot((o * o).astype(BF16), ind) * (1.0 / HEAD_DIM)
    o = o * lax.rsqrt(ms + EPS) * nw_ref[...] * _silu(z)
    o_ref[...] = o.reshape(nbatch, rows, REC_WIDTH).astype(BF16)


def _deltanet(rest, cw, alog, dtb, nw, batch, seq, rows=128):
    rest3 = rest.reshape(batch, seq, REST_COLS)
    body = functools.partial(_dn_body, rows=rows)
    ab0 = (HG_COLS + DN_COLS) // REC_WIDTH
    out = pl.pallas_call(
        body,
        grid=(seq // rows,),
        in_specs=[
            pl.BlockSpec((batch, rows, DN_COLS), lambda r: (0, r, HG_COLS // DN_COLS)),
            pl.BlockSpec((batch, rows, REC_WIDTH), lambda r: (0, r, ab0)),
            pl.BlockSpec((batch, rows, REC_WIDTH), lambda r: (0, r, ab0 + 1)),
            pl.BlockSpec((DN_CONV, DN_CONV_CH), lambda r: (0, 0)),
            pl.BlockSpec((1, REC_WIDTH), lambda r: (0, 0)),
            pl.BlockSpec((1, REC_WIDTH), lambda r: (0, 0)),
            pl.BlockSpec((1, REC_WIDTH), lambda r: (0, 0)),
        ],
        out_specs=pl.BlockSpec((batch, rows, REC_WIDTH), lambda r: (0, r, 0)),
        out_shape=jax.ShapeDtypeStruct((batch, seq, REC_WIDTH), BF16),
        scratch_shapes=[
            pltpu.VMEM((batch, 8, DN_CONV_CH), F32),
            pltpu.VMEM((batch, rows + 8, DN_CONV_CH), F32),
            pltpu.VMEM((batch, REC_WIDTH, REC_WIDTH), F32),
        ],
        compiler_params=_cparams(("arbitrary",)),
        name="gated_deltanet",
    )(rest3, rest3, rest3, cw, alog, dtb, nw)
    return out.reshape(batch * seq, REC_WIDTH)


def _mix_ffn_body(att_ref, hg_ref, dn_ref, h_ref, wo_ref, nw_ref, wup_ref, cw_ref, cb_ref, wdn_ref,
                  fw_ref, o_ref, u_scr, act_scr, *, tm, tf, blocks_per_seq, final):
    first = pl.program_id(0) % blocks_per_seq == 0

    @pl.when(first)
    def _():
        u_scr[:HALO, :] = jnp.zeros((HALO, D_MODEL), BF16)

    @pl.when(jnp.logical_not(first))
    def _():
        u_scr[:HALO, :] = u_scr[tm:tm + HALO, :]

    h1 = h_ref[...] + _dot(att_ref[...], wo_ref[0:ATT_WIDTH, :])
    h1 = h1 + _dot(hg_ref[...], wo_ref[ATT_WIDTH:ATT_WIDTH + REC_WIDTH, :])
    h1 = h1 + _dot(dn_ref[...], wo_ref[ATT_WIDTH + REC_WIDTH:, :])
    o_ref[...] = h1
    u_scr[HALO:, :] = _rms(h1, nw_ref[...]).astype(BF16)
    u = u_scr[...]

    def conv(zz, w, b):
        lo = HALO - (FFN_CONV - 1)
        out = b
        for j in range(FFN_CONV):
            out = out + w[j:j + 1] * zz[lo + j:lo + j + tm]
        return out

    for j in range(D_FF // tf):
        gs = slice(j * tf, (j + 1) * tf)
        vs = slice(D_FF + j * tf, D_FF + (j + 1) * tf)
        gate = conv(_dot(u, wup_ref[:, gs]), cw_ref[:, gs], cb_ref[:, gs])
        val = conv(_dot(u, wup_ref[:, vs]), cw_ref[:, vs], cb_ref[:, vs])
        act_scr[:, gs] = (_silu(gate) * val).astype(BF16)

    act = act_scr[...]
    nstep = 256
    for c0 in range(0, D_MODEL, nstep):
        y = o_ref[:, c0:c0 + nstep] + _dot(act, wdn_ref[:, c0:c0 + nstep])
        o_ref[:, c0:c0 + nstep] = y
    if final:
        o_ref[...] = _rms(o_ref[...], fw_ref[...])


def _mix_ffn(att_o, hg_o, dn_o, h, wo, nw, wup, cw, cb, wdn, fw, seq, final, tm=512, tf=256):
    n = h.shape[0]
    body = functools.partial(_mix_ffn_body, tm=tm, tf=tf, blocks_per_seq=seq // tm, final=final)
    return pl.pallas_call(
        body,
        grid=(n // tm,),
        in_specs=[
            pl.BlockSpec((tm, ATT_WIDTH), lambda i: (i, 0)),
            pl.BlockSpec((tm, REC_WIDTH), lambda i: (i, 0)),
            pl.BlockSpec((tm, REC_WIDTH), lambda i: (i, 0)),
            pl.BlockSpec((tm, D_MODEL), lambda i: (i, 0)),
            _const_spec((D_MODEL, D_MODEL)),
            _const_spec((1, D_MODEL)),
            _const_spec((D_MODEL, 2 * D_FF)),
            _const_spec((FFN_CONV, 2 * D_FF)),
            _const_spec((1, 2 * D_FF)),
            _const_spec((D_FF, D_MODEL)),
            _const_spec((1, D_MODEL)),
        ],
        out_specs=pl.BlockSpec((tm, D_MODEL), lambda i: (i, 0)),
        out_shape=jax.ShapeDtypeStruct((n, D_MODEL), F32),
        scratch_shapes=[
            pltpu.VMEM((HALO + tm, D_MODEL), BF16),
            pltpu.VMEM((tm, D_FF), BF16),
        ],
        compiler_params=_cparams(("arbitrary",)),
        name="mix_ffn",
    )(att_o, hg_o, dn_o, h, wo, nw, wup, cw, cb, wdn, fw)


def _cast_body(x_ref, o_ref):
    o_ref[...] = x_ref[...].astype(BF16)


CAST_BLOCK_BYTES = 8 * 1024 * 1024


def _cast_layer(w, layer):
    _, r, c = w.shape
    rb = r
    while rb * c * 4 > CAST_BLOCK_BYTES and rb % 32 == 0:
        rb //= 2
    return pl.pallas_call(
        _cast_body,
        grid=(r // rb,),
        in_specs=[pl.BlockSpec((None, rb, c), lambda i: (layer, i, 0))],
        out_specs=pl.BlockSpec((rb, c), lambda i: (i, 0)),
        out_shape=jax.ShapeDtypeStruct((r, c), BF16),
        compiler_params=_cparams(("arbitrary",)),
        name="cast_weight",
    )(w)


W_IN_STEP = 512


def _prep_w_in_body(x_ref, o_ref):
    j = pl.program_id(0)
    last = D_IN_WIDE // W_IN_STEP - 1

    @pl.when(j == 0)
    def _():
        o_ref[...] = (x_ref[...] * (ATT_QK_DIM ** -0.5)).T.astype(BF16)

    @pl.when(jnp.logical_and(j > 0, j < last))
    def _():
        o_ref[...] = x_ref[...].T.astype(BF16)

    @pl.when(j == last)
    def _():
        rep = [jnp.broadcast_to(x_ref[r:r + 1, :], (HEAD_DIM, D_MODEL)) for r in range(2 * HEADS)]
        o_ref[...] = jnp.concatenate(rep, axis=0).T.astype(BF16)


def _prep_w_in(w_in, layer):
    assert D_IN - 2 * HEADS == D_IN_WIDE - W_IN_STEP and AB_COLS == W_IN_STEP
    w_t = jnp.swapaxes(w_in, 1, 2)
    return pl.pallas_call(
        _prep_w_in_body,
        grid=(D_IN_WIDE // W_IN_STEP,),
        in_specs=[pl.BlockSpec((None, W_IN_STEP, D_MODEL), lambda j: (layer, j, 0))],
        out_specs=pl.BlockSpec((D_MODEL, W_IN_STEP), lambda j: (0, j)),
        out_shape=jax.ShapeDtypeStruct((D_MODEL, D_IN_WIDE), BF16),
        compiler_params=_cparams(("arbitrary",)),
        name="prep_w_in",
    )(w_t)


def kernel(x, attn_norm_w, w_in, diff_lambda, diff_subln_w, hgrn_lb_logits, hgrn_norm_w, dn_conv_w,
           dn_A_log, dn_dt_bias, dn_norm_w, w_out, ffn_norm_w, ffn_w_up, ffn_conv_w, ffn_conv_b,
           ffn_w_down, final_norm_w):
    batch, seq, _ = x.shape
    h = x.reshape(batch * seq, D_MODEL)
    dn_alog = jnp.repeat(dn_A_log, HEAD_DIM, axis=-1)
    dn_dtb = jnp.repeat(dn_dt_bias, HEAD_DIM, axis=-1)
    hg_nw = jnp.tile(hgrn_norm_w, (1, HEADS))
    dn_nw = jnp.tile(dn_norm_w, (1, HEADS))
    for l in range(DEPTH):
        lambda_init = 0.8 - 0.6 * math.exp(-0.3 * l)
        att, rest = _inproj(h, attn_norm_w[l][None], _prep_w_in(w_in, l))
        att_o = _attention(att, diff_lambda[l], diff_subln_w[l][None], batch, seq, lambda_init)
        hg_o = _hgrn(rest, hgrn_lb_logits, hg_nw[l][None], batch, seq, l)
        dn_o = _deltanet(rest, dn_conv_w[l], dn_alog[l][None], dn_dtb[l][None], dn_nw[l][None],
                         batch, seq)
        h = _mix_ffn(att_o, hg_o, dn_o, h, _cast_layer(w_out, l), ffn_norm_w[l][None],
                     _cast_layer(ffn_w_up, l), ffn_conv_w[l], ffn_conv_b[l][None],
                     _cast_layer(ffn_w_down, l), final_norm_w[None], seq, final=(l == DEPTH - 1))
    return h.reshape(batch, seq, D_MODEL)
```

```python
import functools
import math

import jax
import jax.numpy as jnp
from jax import lax
from jax.experimental import pallas as pl
from jax.experimental.pallas import tpu as pltpu

F32 = jnp.float32
BF16 = jnp.bfloat16
HIGHEST = lax.Precision.HIGHEST

D_MODEL = 1024
DEPTH = 2
ATT_QK_DIM = 64
ATT_V_DIM = 128
ATT_HEADS = 4
ATT_WIDTH = 512
HEADS = 4
HEAD_DIM = 64
REC_WIDTH = HEADS * HEAD_DIM
DN_CONV = 4
DN_CONV_CH = 3 * REC_WIDTH
FFN_CONV = 3
D_FF = 2816
HG_CHUNK = 16
DN_CHUNK = 64
EPS = 1e-6
MASK_NEG = -1e30
F_FLOOR = 1e-30
LOG2E = math.log2(math.e)

ATT_COLS = 3 * ATT_WIDTH
HG_COLS = 4 * REC_WIDTH
DN_COLS = 4 * REC_WIDTH
AB_COLS = 2 * REC_WIDTH
REST_COLS = HG_COLS + DN_COLS + AB_COLS
D_IN = ATT_COLS + HG_COLS + DN_COLS + 2 * HEADS
D_IN_WIDE = ATT_COLS + REST_COLS

VMEM_LIMIT = 56 * 1024 * 1024
LANES = 128
HALO = 16
ROW_STRIDE = 4

def _cparams(sem):
    return pltpu.CompilerParams(dimension_semantics=sem, vmem_limit_bytes=VMEM_LIMIT)


def _const_spec(shape):
    return pl.BlockSpec(shape, lambda *_: (0,) * len(shape), pipeline_mode=pl.Buffered(1))


def _rms(x, w):
    return x * lax.rsqrt(jnp.mean(x * x, axis=-1, keepdims=True) + EPS) * w


def _sigmoid(x):
    return 1.0 / (1.0 + jnp.exp(-x))


def _silu(x):
    return x * _sigmoid(x)


def _dot(a, b, precision=None):
    return jnp.dot(a, b, preferred_element_type=F32, precision=precision)


def _head_of(idx):
    return lax.shift_right_logical(idx, 6)


def _same_head(shape):
    r = lax.broadcasted_iota(jnp.int32, shape, 0)
    c = lax.broadcasted_iota(jnp.int32, shape, 1)
    return _head_of(r) == _head_of(c)


def _cumsum_rows(incl, x):
    hi = x.astype(BF16)
    r1 = x - hi.astype(F32)
    mid = r1.astype(BF16)
    lo = (r1 - mid.astype(F32)).astype(BF16)
    return _dot(incl, hi) + _dot(incl, mid) + _dot(incl, lo)


def _chunk_incl(rows, chunk_shift):
    r = lax.broadcasted_iota(jnp.int32, (rows, rows), 0)
    c = lax.broadcasted_iota(jnp.int32, (rows, rows), 1)
    same = lax.shift_right_logical(r, chunk_shift) == lax.shift_right_logical(c, chunk_shift)
    return jnp.where(same & (c <= r), 1.0, 0.0).astype(BF16)


def _inproj_body(x_ref, nw_ref, w_ref, att_ref, rest_ref):
    u = _rms(x_ref[...], nw_ref[...]).astype(BF16)
    step = 512
    for c0 in range(0, ATT_COLS, step):
        att_ref[:, c0:c0 + step] = _dot(u, w_ref[:, c0:c0 + step]).astype(BF16)
    for c0 in range(0, REST_COLS, step):
        rest_ref[:, c0:c0 + step] = _dot(u, w_ref[:, ATT_COLS + c0:ATT_COLS + c0 + step])


def _inproj(h, nw, w, tm=512):
    n = h.shape[0]
    return pl.pallas_call(
        _inproj_body,
        grid=(n // tm,),
        in_specs=[
            pl.BlockSpec((tm, D_MODEL), lambda i: (i, 0)),
            _const_spec((1, D_MODEL)),
            _const_spec((D_MODEL, D_IN_WIDE)),
        ],
        out_specs=[
            pl.BlockSpec((tm, ATT_COLS), lambda i: (i, 0)),
            pl.BlockSpec((tm, REST_COLS), lambda i: (i, 0)),
        ],
        out_shape=[
            jax.ShapeDtypeStruct((n, ATT_COLS), BF16),
            jax.ShapeDtypeStruct((n, REST_COLS), F32),
        ],
        compiler_params=_cparams(("arbitrary",)),
        name="inproj",
    )(h, nw, w)


def _attn_body(lamp_ref, subw_ref, q_ref, k_ref, v_ref, o_ref,
               vt_scr, qt_scr, sa_scr, sb_scr, ma_scr, mb_scr, pa_scr, pb_scr, m_scr, l_scr, al_scr,
               acc_scr, *, tq, tk, lambda_init):
    i = pl.program_id(2)
    nblk = vt_scr.shape[0] - 1

    @pl.when(i == 0)
    def _():
        for c in range(nblk):
            vt_scr[c] = v_ref[c * tk:(c + 1) * tk, :].astype(F32).T.astype(BF16)
        vt_scr[nblk] = jnp.zeros(vt_scr.shape[1:], BF16)
        pb_scr[...] = jnp.zeros(pb_scr.shape, BF16)

    qt = q_ref[...].astype(F32).T * LOG2E
    first = lax.broadcasted_iota(jnp.int32, qt.shape, 0) < ATT_QK_DIM
    qt_scr[:, :tq] = jnp.where(first, qt, 0.0).astype(BF16)
    qt_scr[:, tq:] = jnp.where(first, 0.0, qt).astype(BF16)
    m_scr[...] = jnp.full(m_scr.shape, MASK_NEG, F32)
    l_scr[...] = jnp.zeros(l_scr.shape, F32)
    al_scr[...] = jnp.ones(al_scr.shape, F32)
    acc_scr[...] = jnp.zeros(acc_scr.shape, F32)

    def put_scores(dst, j):
        kj = k_ref[pl.ds(pl.multiple_of(j * tk, tk), tk), :]
        s = _dot(kj, qt_scr[...])
        dst[0][...] = s
        dst[1][...] = jnp.max(s, axis=0, keepdims=True)

    def phase(j, s_cur, p_cur, s_nxt, p_prv, masked, prefetch=True):
        if prefetch:
            put_scores(s_nxt, j + 1)
        pv = _dot(vt_scr[jnp.where(j == 0, nblk, j - 1)], p_prv[...])
        acc_scr[...] = al_scr[...] * acc_scr[...] + pv
        s = s_cur[0][...]
        if masked:
            kpos = lax.broadcasted_iota(jnp.int32, s.shape, 0) + (j * tk - i * tq)
            qpos = lax.broadcasted_iota(jnp.int32, s.shape, 1)
            qpos = jnp.where(qpos >= tq, qpos - tq, qpos)
            s = jnp.where(kpos <= qpos, s, MASK_NEG)
            m_tile = jnp.max(s, axis=0, keepdims=True)
        else:
            m_tile = s_cur[1][...]
        m_old = m_scr[...]
        m_new = jnp.maximum(m_old, m_tile)
        alpha = jnp.exp2(m_old - m_new)
        p = jnp.exp2(s - m_new)
        l_scr[...] = alpha * l_scr[...] + jnp.sum(p, axis=0, keepdims=True)
        m_scr[...] = m_new
        p_cur[...] = p.astype(BF16)
        al_scr[...] = alpha

    def finish(j, p_cur):
        acc_scr[...] = al_scr[...] * acc_scr[...] + _dot(vt_scr[j], p_cur[...])

    sa = (sa_scr, ma_scr)
    sb = (sb_scr, mb_scr)
    put_scores(sa, 0)

    def pair(jj, carry):
        phase(2 * jj, sa, pa_scr, sb, pb_scr, False)
        phase(2 * jj + 1, sb, pb_scr, sa, pa_scr, False)
        return carry

    lax.fori_loop(0, lax.shift_right_logical(i, 1), pair, 0)

    @pl.when((i & 1) == 0)
    def _():
        phase(i, sa, pa_scr, sb, pb_scr, True, prefetch=False)
        finish(i, pa_scr)

    @pl.when((i & 1) == 1)
    def _():
        phase(i - 1, sa, pa_scr, sb, pb_scr, False)
        phase(i, sb, pb_scr, sa, pa_scr, True, prefetch=False)
        finish(i, pb_scr)

    acc = acc_scr[...]
    lp = lamp_ref[...]
    lam = (jnp.exp(jnp.sum(lp[0:1] * lp[1:2], axis=-1, keepdims=True))
           - jnp.exp(jnp.sum(lp[2:3] * lp[3:4], axis=-1, keepdims=True)) + lambda_init)
    rl = 1.0 / l_scr[...]
    ot = acc[:, :tq] * rl[:, :tq] - lam * (acc[:, tq:] * rl[:, tq:])
    ot = ot * (lax.rsqrt(jnp.mean(ot * ot, axis=0, keepdims=True) + EPS) * (1.0 - lambda_init))
    o_ref[...] = (ot.T * subw_ref[...]).astype(BF16)


def _attention(att, lamp, subw, batch, seq, lambda_init, tq=512):
    n = att.shape[0]
    nq = seq // tq
    tk = tq
    body = functools.partial(_attn_body, tq=tq, tk=tk, lambda_init=lambda_init)
    return pl.pallas_call(
        body,
        grid=(batch, ATT_HEADS, nq),
        in_specs=[
            pl.BlockSpec((4, ATT_QK_DIM), lambda b, h, i: (0, 0)),
            pl.BlockSpec((1, ATT_V_DIM), lambda b, h, i: (0, 0)),
            pl.BlockSpec((tq, LANES), lambda b, h, i: (b * nq + i, h)),
            pl.BlockSpec((seq, LANES), lambda b, h, i: (b, ATT_HEADS + h)),
            pl.BlockSpec((seq, LANES), lambda b, h, i: (b, 2 * ATT_HEADS + h)),
        ],
        out_specs=pl.BlockSpec((tq, LANES), lambda b, h, i: (b * nq + i, h)),
        out_shape=jax.ShapeDtypeStruct((n, ATT_WIDTH), BF16),
        scratch_shapes=[
            pltpu.VMEM((seq // tk + 1, LANES, tk), BF16),
            pltpu.VMEM((LANES, 2 * tq), BF16),
            pltpu.VMEM((tk, 2 * tq), F32),
            pltpu.VMEM((tk, 2 * tq), F32),
            pltpu.VMEM((1, 2 * tq), F32),
            pltpu.VMEM((1, 2 * tq), F32),
            pltpu.VMEM((tk, 2 * tq), BF16),
            pltpu.VMEM((tk, 2 * tq), BF16),
            pltpu.VMEM((1, 2 * tq), F32),
            pltpu.VMEM((1, 2 * tq), F32),
            pltpu.VMEM((1, 2 * tq), F32),
            pltpu.VMEM((LANES, 2 * tq), F32),
        ],
        compiler_params=_cparams(("arbitrary", "arbitrary", "arbitrary")),
        name="diff_attention",
    )(lamp, subw, att, att, att)


def _hgrn_body(x_ref, lbl_ref, nw_ref, o_ref, st_scr, b_scr, q_scr, k_scr, v_scr,
               qd_scr, oi_scr, *, layer, rows):
    @pl.when(pl.program_id(1) == 0)
    def _():
        st_scr[...] = jnp.zeros(st_scr.shape, F32)

    lg = lbl_ref[...]
    e = jnp.exp(lg - jnp.max(lg, axis=0, keepdims=True))
    sm = e / jnp.sum(e, axis=0, keepdims=True)
    lb = jnp.zeros((1, REC_WIDTH), F32)
    for i in range(1, layer + 1):
        lb = lb + sm[i:i + 1]

    x = x_ref[...]
    q = x[:, 0:REC_WIDTH]
    fp = x[:, REC_WIDTH:2 * REC_WIDTH]
    iv = x[:, 2 * REC_WIDTH:3 * REC_WIDTH]
    gate = x[:, 3 * REC_WIDTH:4 * REC_WIDTH]
    qf = _silu(q)
    f = lb + (1.0 - lb) * _sigmoid(fp)
    logf = jnp.log2(jnp.maximum(f, F_FLOOR))
    kf = (1.0 - lb) * _sigmoid(-fp)

    b = _cumsum_rows(_chunk_incl(rows, 4), logf)
    def put(ref, val):
        ref[0] = val[:, :LANES]
        ref[1] = val[:, LANES:]

    def rows_at(ref, sl):
        return jnp.concatenate([ref[0, sl, :], ref[1, sl, :]], axis=1)

    def row_rep(ref, r):
        return rows_at(ref, pl.ds(r, HG_CHUNK, stride=0))

    put(b_scr, b)
    put(k_scr, kf)
    put(v_scr, iv)
    q_scr[...] = qf
    qd_scr[...] = qf * jnp.exp2(b)

    same = _same_head((REC_WIDTH, REC_WIDTH))
    ind = jnp.where(same, 1.0, 0.0).astype(BF16)
    trow = lax.broadcasted_iota(jnp.int32, (HG_CHUNK, REC_WIDTH), 0)

    def chunk(c, carry):
        base = c * HG_CHUNK
        sl = pl.ds(base, HG_CHUNK)
        b_c = rows_at(b_scr, sl)
        q_c = q_scr[sl, :]
        b_last = row_rep(b_scr, base + HG_CHUNK - 1)
        kd_c = rows_at(k_scr, sl) * jnp.exp2(b_last - b_c)
        st = st_scr[...]
        o_inter = lax.dot_general(qd_scr[sl, :].astype(BF16), st.astype(BF16),
                                  (((1,), (1,)), ((), ())), preferred_element_type=F32)
        kvt = lax.dot_general(rows_at(v_scr, sl).astype(BF16), kd_c.astype(BF16),
                              (((0,), (0,)), ((), ())), preferred_element_type=F32)
        st_scr[...] = st * jnp.exp2(b_last[0:1]) + jnp.where(same, kvt, 0.0)
        slabs = []
        for s in range(HG_CHUNK):
            rel = jnp.where(trow >= s, b_c - row_rep(b_scr, base + s), MASK_NEG)
            slabs.append((jnp.exp2(rel) * q_c * row_rep(k_scr, base + s)).astype(BF16))
        a = _dot(jnp.concatenate(slabs, axis=0), ind)
        o_intra = a[0:HG_CHUNK] * row_rep(v_scr, base)
        for s in range(1, HG_CHUNK):
            o_intra = o_intra + a[s * HG_CHUNK:(s + 1) * HG_CHUNK] * row_rep(v_scr, base + s)
        oi_scr[sl, :] = o_inter + o_intra
        return carry

    for c in range(rows // HG_CHUNK):
        chunk(c, 0)

    o = oi_scr[...]
    ms = _dot((o * o).astype(BF16), ind) * (1.0 / HEAD_DIM)
    y = o * lax.rsqrt(ms + EPS) * nw_ref[...] * _silu(gate)
    o_ref[...] = y.astype(BF16)


def _hgrn(rest, lb_logits, nw, batch, seq, layer, rows=256):
    n = rest.shape[0]
    nb = seq // rows
    body = functools.partial(_hgrn_body, layer=layer, rows=rows)
    rec = pltpu.VMEM((rows, REC_WIDTH), F32)
    slab = pltpu.VMEM((REC_WIDTH // LANES, rows, LANES), F32)
    return pl.pallas_call(
        body,
        grid=(batch, nb),
        in_specs=[
            pl.BlockSpec((rows, HG_COLS), lambda b, r: (b * nb + r, 0)),
            pl.BlockSpec((DEPTH, REC_WIDTH), lambda b, r: (0, 0)),
            pl.BlockSpec((1, REC_WIDTH), lambda b, r: (0, 0)),
        ],
        out_specs=pl.BlockSpec((rows, REC_WIDTH), lambda b, r: (b * nb + r, 0)),
        out_shape=jax.ShapeDtypeStruct((n, REC_WIDTH), BF16),
        scratch_shapes=[pltpu.VMEM((REC_WIDTH, REC_WIDTH), F32),
                        slab, rec, slab, slab, rec, rec],
        compiler_params=_cparams(("arbitrary", "arbitrary")),
        name="hgrn2",
    )(rest, lb_logits, nw)


def _dn_body(x_ref, a_ref, b_ref, cw_ref, alog_ref, dtb_ref, nw_ref, o_ref,
             carry_scr, xs_scr, ys_scr, s_scr, *, rows):
    nbatch = x_ref.shape[0]
    nchunk = rows // DN_CHUNK
    flat = nbatch * rows

    @pl.when(pl.program_id(0) == 0)
    def _():
        carry_scr[...] = jnp.zeros(carry_scr.shape, F32)
        s_scr[...] = jnp.zeros(s_scr.shape, F32)

    cw = cw_ref[...]
    quarter = rows // ROW_STRIDE
    ys = []
    for bi in range(nbatch):
        for s in range(DN_CONV_CH // LANES):
            lanes = slice(s * LANES, (s + 1) * LANES)
            xs_scr[bi, s, 0:8, :] = carry_scr[bi, s]
            xs_scr[bi, s, 8:, :] = x_ref[bi, :, lanes]
            carry_scr[bi, s] = x_ref[bi, rows - 8:rows, lanes]
            for r in range(ROW_STRIDE):
                acc = cw[0:1, lanes] * xs_scr[bi, s, pl.ds(5 + r, quarter, stride=ROW_STRIDE), :]
                for j in range(1, DN_CONV):
                    acc = acc + cw[j:j + 1, lanes] * xs_scr[bi, s, pl.ds(5 + j + r, quarter,
                                                                         stride=ROW_STRIDE), :]
                ys_scr[bi, s, pl.ds(r, quarter, stride=ROW_STRIDE), :] = _silu(acc)
        ys.append(jnp.concatenate([ys_scr[bi, s] for s in range(DN_CONV_CH // LANES)], axis=1))
    y = jnp.concatenate(ys, axis=0)
    z = x_ref[...].reshape(flat, DN_COLS)[:, DN_CONV_CH:]
    qc = y[:, 0:REC_WIDTH]
    kc = y[:, REC_WIDTH:2 * REC_WIDTH]
    vc = y[:, 2 * REC_WIDTH:]

    same = _same_head((REC_WIDTH, REC_WIDTH))
    ind = jnp.where(same, 1.0, 0.0).astype(BF16)
    qn = qc * lax.rsqrt(_dot((qc * qc).astype(BF16), ind) + EPS) * (HEAD_DIM ** -0.5)
    kn = kc * lax.rsqrt(_dot((kc * kc).astype(BF16), ind) + EPS)

    sp_in = a_ref[...].reshape(flat, REC_WIDTH) + dtb_ref[...]
    softplus = jnp.maximum(sp_in, 0.0) + jnp.log1p(jnp.exp(-jnp.abs(sp_in)))
    g = -jnp.exp(alog_ref[...]) * softplus
    beta = _sigmoid(b_ref[...].reshape(flat, REC_WIDTH))
    kb = kn * beta
    vb = vc * beta

    crow = lax.broadcasted_iota(jnp.int32, (DN_CHUNK, REC_WIDTH), 0)
    slane = lax.broadcasted_iota(jnp.int32, (DN_CHUNK, REC_WIDTH), 1) & (HEAD_DIM - 1)
    diag = crow == slane
    eye = jnp.where(diag, 1.0, 0.0).astype(F32)
    incl = _chunk_incl(DN_CHUNK, 6)

    def bdiag(v):
        v16 = v.astype(BF16)
        return jnp.where(same, jnp.concatenate([v16] * HEADS, axis=0), jnp.zeros((), BF16))

    chunks = [(bi, n) for n in range(nchunk) for bi in range(nbatch)]
    rows_of = {}
    for bi, n in chunks:
        start = bi * rows + n * DN_CHUNK
        rows_of[(bi, n)] = slice(start, start + DN_CHUNK)
    gc, g_row, kt_bd, a_kk, a_qk, xinv, pw, u, w = {}, {}, {}, {}, {}, {}, {}, {}, {}
    for ch in chunks:
        gc[ch] = _cumsum_rows(incl, g[rows_of[ch]])
    for ch in chunks:
        g_row[ch] = jnp.sum(jnp.where(diag, gc[ch], 0.0), axis=0, keepdims=True)
        kt = jnp.concatenate([kn[rows_of[ch]]] * HEADS, axis=0).T
        kt_bd[ch] = jnp.where(same, kt, 0.0)
    for ch in chunks:
        sl = rows_of[ch]
        decay = jnp.exp(jnp.where(slane <= crow, gc[ch] - g_row[ch], MASK_NEG))
        sc = _dot(jnp.concatenate([kb[sl], qn[sl]], axis=0).astype(BF16), kt_bd[ch].astype(BF16))
        a_kk[ch] = jnp.where(slane < crow, sc[:DN_CHUNK] * decay, 0.0)
        a_qk[ch] = sc[DN_CHUNK:] * decay
    for ch in chunks:
        xinv[ch] = eye - a_kk[ch]
        pw[ch] = _dot(a_kk[ch].astype(BF16), bdiag(a_kk[ch]))
    for _ in range(4):
        for ch in chunks:
            r = _dot(jnp.concatenate([xinv[ch], pw[ch]], axis=0).astype(BF16), bdiag(pw[ch]))
            xinv[ch] = xinv[ch] + r[:DN_CHUNK]
            pw[ch] = r[DN_CHUNK:]
    for ch in chunks:
        sl = rows_of[ch]
        xi = (xinv[ch] + _dot(xinv[ch].astype(BF16), bdiag(pw[ch]))).astype(BF16)
        u[ch] = _dot(xi, bdiag(vb[sl]))
        w[ch] = _dot(xi, bdiag(kb[sl] * jnp.exp(gc[ch])))
    outs = {}
    state = [s_scr[bi] for bi in range(nbatch)]
    for n in range(nchunk):
        for bi in range(nbatch):
            ch = (bi, n)
            q_dec = qn[rows_of[ch]] * jnp.exp(gc[ch])
            r1 = _dot(jnp.concatenate([w[ch], q_dec], axis=0).astype(BF16), state[bi].astype(BF16))
            v_new = u[ch] - r1[:DN_CHUNK]
            gl_row = gc[ch][DN_CHUNK - 1:DN_CHUNK]
            kdt_bd = kt_bd[ch] * jnp.exp(gl_row - g_row[ch])
            r2 = _dot(jnp.concatenate([a_qk[ch], kdt_bd], axis=0).astype(BF16), bdiag(v_new))
            outs[ch] = r1[DN_CHUNK:] + r2[:DN_CHUNK]
            state[bi] = state[bi] * jnp.exp(gl_row) + r2[DN_CHUNK:]
    for bi in range(nbatch):
        s_scr[bi] = state[bi]

    o = jnp.concatenate([outs[(bi, n)] for bi in range(nbatch) for n in range(nchunk)], axis=0)
    ms = _dot((o * o).astype(BF16), ind) * (1.0 / HEAD_DIM)
    o = o * lax.rsqrt(ms + EPS) * nw_ref[...] * _silu(z)
    o_ref[...] = o.reshape(nbatch, rows, REC_WIDTH).astype(BF16)


def _deltanet(rest, cw, alog, dtb, nw, batch, seq, rows=256):
    rest3 = rest.reshape(batch, seq, REST_COLS)
    body = functools.partial(_dn_body, rows=rows)
    ab0 = (HG_COLS + DN_COLS) // REC_WIDTH
    out = pl.pallas_call(
        body,
        grid=(seq // rows,),
        in_specs=[
            pl.BlockSpec((batch, rows, DN_COLS), lambda r: (0, r, HG_COLS // DN_COLS)),
            pl.BlockSpec((batch, rows, REC_WIDTH), lambda r: (0, r, ab0)),
            pl.BlockSpec((batch, rows, REC_WIDTH), lambda r: (0, r, ab0 + 1)),
            pl.BlockSpec((DN_CONV, DN_CONV_CH), lambda r: (0, 0)),
            pl.BlockSpec((1, REC_WIDTH), lambda r: (0, 0)),
            pl.BlockSpec((1, REC_WIDTH), lambda r: (0, 0)),
            pl.BlockSpec((1, REC_WIDTH), lambda r: (0, 0)),
        ],
        out_specs=pl.BlockSpec((batch, rows, REC_WIDTH), lambda r: (0, r, 0)),
        out_shape=jax.ShapeDtypeStruct((batch, seq, REC_WIDTH), BF16),
        scratch_shapes=[
            pltpu.VMEM((batch, DN_CONV_CH // LANES, 8, LANES), F32),
            pltpu.VMEM((batch, DN_CONV_CH // LANES, rows + 8, LANES), F32),
            pltpu.VMEM((batch, DN_CONV_CH // LANES, rows, LANES), F32),
            pltpu.VMEM((batch, REC_WIDTH, REC_WIDTH), F32),
        ],
        compiler_params=_cparams(("arbitrary",)),
        name="gated_deltanet",
    )(rest3, rest3, rest3, cw, alog, dtb, nw)
    return out.reshape(batch * seq, REC_WIDTH)


def _mix_ffn_body(att_ref, hg_ref, dn_ref, h_ref, wo_ref, nw_ref, wup_ref, cw_ref, cb_ref, wdn_ref,
                  fw_ref, o_ref, u_scr, act_scr, zg_scr, zv_scr, back_scr,
                  *, tm, tf, blocks_per_seq, final):
    first = pl.program_id(0) % blocks_per_seq == 0

    @pl.when(first)
    def _():
        u_scr[:HALO, :] = jnp.zeros((HALO, D_MODEL), BF16)

    @pl.when(jnp.logical_not(first))
    def _():
        u_scr[:HALO, :] = u_scr[tm:tm + HALO, :]

    h1 = h_ref[...] + _dot(att_ref[...], wo_ref[0:ATT_WIDTH, :])
    h1 = h1 + _dot(hg_ref[...], wo_ref[ATT_WIDTH:ATT_WIDTH + REC_WIDTH, :])
    h1 = h1 + _dot(dn_ref[...], wo_ref[ATT_WIDTH + REC_WIDTH:, :])
    o_ref[...] = h1
    u_scr[HALO:, :] = _rms(h1, nw_ref[...]).astype(BF16)
    u = u_scr[...]

    quarter = tm // ROW_STRIDE

    def park(z_scr, z):
        for s in range(tf // LANES):
            z_scr[s] = z[:, s * LANES:(s + 1) * LANES]

    def rows_back(z_scr, d):
        cols = []
        for s in range(tf // LANES):
            cols.append(jnp.concatenate(
                [z_scr[s, pl.ds(HALO + r - d, quarter, stride=ROW_STRIDE), :] for r in range(ROW_STRIDE)],
                axis=0))
        return jnp.concatenate(cols, axis=1)

    def conv(z_scr, w, b):
        out = b
        for j in range(FFN_CONV):
            out = out + w[j:j + 1] * rows_back(z_scr, FFN_CONV - 1 - j)
        return out

    for j in range(D_FF // tf):
        gs = slice(j * tf, (j + 1) * tf)
        vs = slice(D_FF + j * tf, D_FF + (j + 1) * tf)
        park(zg_scr, _dot(u, wup_ref[:, gs]))
        park(zv_scr, _dot(u, wup_ref[:, vs]))
        gate = conv(zg_scr, cw_ref[:, gs], cb_ref[:, gs])
        val = conv(zv_scr, cw_ref[:, vs], cb_ref[:, vs])
        act_scr[:, gs] = (_silu(gate) * val).astype(BF16)

    act = act_scr[...]
    nstep = 256
    for c0 in range(0, D_MODEL, nstep):
        y = _dot(act, wdn_ref[:, c0:c0 + nstep])
        for s in range(nstep // LANES):
            for r in range(ROW_STRIDE):
                back_scr[c0 // LANES + s, pl.ds(r, quarter, stride=ROW_STRIDE), :] = (
                    y[r * quarter:(r + 1) * quarter, s * LANES:(s + 1) * LANES])
    for s in range(D_MODEL // LANES):
        o_ref[:, s * LANES:(s + 1) * LANES] = o_ref[:, s * LANES:(s + 1) * LANES] + back_scr[s]
    if final:
        o_ref[...] = _rms(o_ref[...], fw_ref[...])


def _mix_ffn(att_o, hg_o, dn_o, h, wo, nw, wup, cw, cb, wdn, fw, seq, final, tm=512, tf=256):
    n = h.shape[0]
    body = functools.partial(_mix_ffn_body, tm=tm, tf=tf, blocks_per_seq=seq // tm, final=final)
    return pl.pallas_call(
        body,
        grid=(n // tm,),
        in_specs=[
            pl.BlockSpec((tm, ATT_WIDTH), lambda i: (i, 0)),
            pl.BlockSpec((tm, REC_WIDTH), lambda i: (i, 0)),
            pl.BlockSpec((tm, REC_WIDTH), lambda i: (i, 0)),
            pl.BlockSpec((tm, D_MODEL), lambda i: (i, 0)),
            _const_spec((D_MODEL, D_MODEL)),
            _const_spec((1, D_MODEL)),
            _const_spec((D_MODEL, 2 * D_FF)),
            _const_spec((FFN_CONV, 2 * D_FF)),
            _const_spec((1, 2 * D_FF)),
            _const_spec((D_FF, D_MODEL)),
            _const_spec((1, D_MODEL)),
        ],
        out_specs=pl.BlockSpec((tm, D_MODEL), lambda i: (i, 0)),
        out_shape=jax.ShapeDtypeStruct((n, D_MODEL), F32),
        scratch_shapes=[
            pltpu.VMEM((HALO + tm, D_MODEL), BF16),
            pltpu.VMEM((tm, D_FF), BF16),
            pltpu.VMEM((tf // LANES, HALO + tm, LANES), F32),
            pltpu.VMEM((tf // LANES, HALO + tm, LANES), F32),
            pltpu.VMEM((D_MODEL // LANES, tm, LANES), F32),
        ],
        compiler_params=_cparams(("arbitrary",)),
        name="mix_ffn",
    )(att_o, hg_o, dn_o, h, wo, nw, wup, cw, cb, wdn, fw)


def _cast_body(x_ref, o_ref):
    o_ref[...] = x_ref[...].astype(BF16)


CAST_BLOCK_BYTES = 8 * 1024 * 1024


def _cast_layer(w, layer):
    _, r, c = w.shape
    rb = r
    while rb * c * 4 > CAST_BLOCK_BYTES and rb % 32 == 0:
        rb //= 2
    return pl.pallas_call(
        _cast_body,
        grid=(r // rb,),
        in_specs=[pl.BlockSpec((None, rb, c), lambda i: (layer, i, 0))],
        out_specs=pl.BlockSpec((rb, c), lambda i: (i, 0)),
        out_shape=jax.ShapeDtypeStruct((r, c), BF16),
        compiler_params=_cparams(("arbitrary",)),
        name="cast_weight",
    )(w)


W_IN_STEP = 512


def _prep_w_in_body(x_ref, o_ref):
    j = pl.program_id(0)
    last = D_IN_WIDE // W_IN_STEP - 1

    @pl.when(j == 0)
    def _():
        o_ref[...] = (x_ref[...] * (ATT_QK_DIM ** -0.5)).T.astype(BF16)

    @pl.when(jnp.logical_and(j > 0, j < last))
    def _():
        o_ref[...] = x_ref[...].T.astype(BF16)

    @pl.when(j == last)
    def _():
        rep = [jnp.broadcast_to(x_ref[r:r + 1, :], (HEAD_DIM, D_MODEL)) for r in range(2 * HEADS)]
        o_ref[...] = jnp.concatenate(rep, axis=0).T.astype(BF16)


def _prep_w_in(w_in, layer):
    assert D_IN - 2 * HEADS == D_IN_WIDE - W_IN_STEP and AB_COLS == W_IN_STEP
    w_t = jnp.swapaxes(w_in, 1, 2)
    return pl.pallas_call(
        _prep_w_in_body,
        grid=(D_IN_WIDE // W_IN_STEP,),
        in_specs=[pl.BlockSpec((None, W_IN_STEP, D_MODEL), lambda j: (layer, j, 0))],
        out_specs=pl.BlockSpec((D_MODEL, W_IN_STEP), lambda j: (0, j)),
        out_shape=jax.ShapeDtypeStruct((D_MODEL, D_IN_WIDE), BF16),
        compiler_params=_cparams(("arbitrary",)),
        name="prep_w_in",
    )(w_t)


def kernel(x, attn_norm_w, w_in, diff_lambda, diff_subln_w, hgrn_lb_logits, hgrn_norm_w, dn_conv_w,
           dn_A_log, dn_dt_bias, dn_norm_w, w_out, ffn_norm_w, ffn_w_up, ffn_conv_w, ffn_conv_b,
           ffn_w_down, final_norm_w):
    batch, seq, _ = x.shape
    h = x.reshape(batch * seq, D_MODEL)
    dn_alog = jnp.repeat(dn_A_log, HEAD_DIM, axis=-1)
    dn_dtb = jnp.repeat(dn_dt_bias, HEAD_DIM, axis=-1)
    hg_nw = jnp.tile(hgrn_norm_w, (1, HEADS))
    dn_nw = jnp.tile(dn_norm_w, (1, HEADS))
    for l in range(DEPTH):
        lambda_init = 0.8 - 0.6 * math.exp(-0.3 * l)
        att, rest = _inproj(h, attn_norm_w[l][None], _prep_w_in(w_in, l))
        att_o = _attention(att, diff_lambda[l], diff_subln_w[l][None], batch, seq, lambda_init)
        hg_o = _hgrn(rest, hgrn_lb_logits, hg_nw[l][None], batch, seq, l)
        dn_o = _deltanet(rest, dn_conv_w[l], dn_alog[l][None], dn_dtb[l][None], dn_nw[l][None],
                         batch, seq)
        h = _mix_ffn(att_o, hg_o, dn_o, h, _cast_layer(w_out, l), ffn_norm_w[l][None],
                     _cast_layer(ffn_w_up, l), ffn_conv_w[l], ffn_conv_b[l][None],
                     _cast_layer(ffn_w_down, l), final_norm_w[None], seq, final=(l == DEPTH - 1))
    return h.reshape(batch, seq, D_MODEL)
```

```python
import functools
import math

import jax
import jax.numpy as jnp
from jax import lax
from jax.experimental import pallas as pl
from jax.experimental.pallas import tpu as pltpu

F32 = jnp.float32
BF16 = jnp.bfloat16

D_MODEL = 1024
DEPTH = 2
ATT_QK_DIM = 64
ATT_V_DIM = 128
ATT_HEADS = 4
ATT_WIDTH = 512
HEADS = 4
HEAD_DIM = 64
REC_WIDTH = HEADS * HEAD_DIM
DN_CONV = 4
DN_CONV_CH = 3 * REC_WIDTH
FFN_CONV = 3
D_FF = 2816
HG_CHUNK = 16
DN_CHUNK = 64
EPS = 1e-6
MASK_NEG = -1e30
F_FLOOR = 1e-30
LOG2E = math.log2(math.e)

ATT_COLS = 3 * ATT_WIDTH
HG_COLS = 4 * REC_WIDTH
DN_COLS = 4 * REC_WIDTH
AB_COLS = 2 * REC_WIDTH
REST_COLS = HG_COLS + DN_COLS + AB_COLS
D_IN = ATT_COLS + HG_COLS + DN_COLS + 2 * HEADS
D_IN_WIDE = ATT_COLS + REST_COLS

VMEM_LIMIT = 56 * 1024 * 1024
LANES = 128
HALO = 16
ROW_STRIDE = 4

def _cparams(sem):
    return pltpu.CompilerParams(dimension_semantics=sem, vmem_limit_bytes=VMEM_LIMIT)


def _const_spec(shape):
    return pl.BlockSpec(shape, lambda *_: (0,) * len(shape), pipeline_mode=pl.Buffered(1))


def _rms(x, w):
    return x * lax.rsqrt(jnp.mean(x * x, axis=-1, keepdims=True) + EPS) * w


def _sigmoid(x):
    return 1.0 / (1.0 + jnp.exp(-x))


def _silu(x):
    return x * _sigmoid(x)


def _dot(a, b):
    return jnp.dot(a, b, preferred_element_type=F32)


def _head_of(idx):
    return lax.shift_right_logical(idx, 6)


def _same_head(shape):
    r = lax.broadcasted_iota(jnp.int32, shape, 0)
    c = lax.broadcasted_iota(jnp.int32, shape, 1)
    return _head_of(r) == _head_of(c)


def _cumsum_rows(incl, x):
    hi = x.astype(BF16)
    r1 = x - hi.astype(F32)
    mid = r1.astype(BF16)
    lo = (r1 - mid.astype(F32)).astype(BF16)
    return _dot(incl, hi) + _dot(incl, mid) + _dot(incl, lo)


def _chunk_incl(rows, chunk_shift):
    r = lax.broadcasted_iota(jnp.int32, (rows, rows), 0)
    c = lax.broadcasted_iota(jnp.int32, (rows, rows), 1)
    same = lax.shift_right_logical(r, chunk_shift) == lax.shift_right_logical(c, chunk_shift)
    return jnp.where(same & (c <= r), 1.0, 0.0).astype(BF16)


def _inproj_body(x_ref, nw_ref, w_ref, att_ref, rest_ref):
    u = _rms(x_ref[...], nw_ref[...]).astype(BF16)
    step = 512
    for c0 in range(0, ATT_COLS, step):
        att_ref[:, c0:c0 + step] = _dot(u, w_ref[:, c0:c0 + step]).astype(BF16)
    for c0 in range(0, REST_COLS, step):
        rest_ref[:, c0:c0 + step] = _dot(u, w_ref[:, ATT_COLS + c0:ATT_COLS + c0 + step])


def _inproj(h, nw, w, tm=512):
    n = h.shape[0]
    return pl.pallas_call(
        _inproj_body,
        grid=(n // tm,),
        in_specs=[
            pl.BlockSpec((tm, D_MODEL), lambda i: (i, 0)),
            _const_spec((1, D_MODEL)),
            _const_spec((D_MODEL, D_IN_WIDE)),
        ],
        out_specs=[
            pl.BlockSpec((tm, ATT_COLS), lambda i: (i, 0)),
            pl.BlockSpec((tm, REST_COLS), lambda i: (i, 0)),
        ],
        out_shape=[
            jax.ShapeDtypeStruct((n, ATT_COLS), BF16),
            jax.ShapeDtypeStruct((n, REST_COLS), F32),
        ],
        compiler_params=_cparams(("arbitrary",)),
        name="inproj",
    )(h, nw, w)


def _attn_body(lamp_ref, subw_ref, q_ref, k_ref, v_ref, o_ref,
               vt_scr, qt_scr, sa_scr, sb_scr, ma_scr, mb_scr, pa_scr, pb_scr, m_scr, l_scr, al_scr,
               acc_scr, *, tq, tk, lambda_init):
    i = pl.program_id(2)
    nblk = vt_scr.shape[0] - 1

    @pl.when(i == 0)
    def _():
        for c in range(nblk):
            vt_scr[c] = v_ref[c * tk:(c + 1) * tk, :].astype(F32).T.astype(BF16)
        vt_scr[nblk] = jnp.zeros(vt_scr.shape[1:], BF16)
        pb_scr[...] = jnp.zeros(pb_scr.shape, BF16)

    qt = q_ref[...].astype(F32).T * LOG2E
    first = lax.broadcasted_iota(jnp.int32, qt.shape, 0) < ATT_QK_DIM
    qt_scr[:, :tq] = jnp.where(first, qt, 0.0).astype(BF16)
    qt_scr[:, tq:] = jnp.where(first, 0.0, qt).astype(BF16)
    m_scr[...] = jnp.full(m_scr.shape, MASK_NEG, F32)
    l_scr[...] = jnp.zeros(l_scr.shape, F32)
    al_scr[...] = jnp.ones(al_scr.shape, F32)
    acc_scr[...] = jnp.zeros(acc_scr.shape, F32)

    def put_scores(dst, j):
        kj = k_ref[pl.ds(pl.multiple_of(j * tk, tk), tk), :]
        s = _dot(kj, qt_scr[...])
        dst[0][...] = s
        dst[1][...] = jnp.max(s, axis=0, keepdims=True)

    def phase(j, s_cur, p_cur, s_nxt, p_prv, masked, prefetch=True):
        if prefetch:
            put_scores(s_nxt, j + 1)
        pv = _dot(vt_scr[jnp.where(j == 0, nblk, j - 1)], p_prv[...])
        acc_scr[...] = al_scr[...] * acc_scr[...] + pv
        s = s_cur[0][...]
        if masked:
            kpos = lax.broadcasted_iota(jnp.int32, s.shape, 0) + (j * tk - i * tq)
            qpos = lax.broadcasted_iota(jnp.int32, s.shape, 1)
            qpos = jnp.where(qpos >= tq, qpos - tq, qpos)
            s = jnp.where(kpos <= qpos, s, MASK_NEG)
            m_tile = jnp.max(s, axis=0, keepdims=True)
        else:
            m_tile = s_cur[1][...]
        m_old = m_scr[...]
        m_new = jnp.maximum(m_old, m_tile)
        alpha = jnp.exp2(m_old - m_new)
        p = jnp.exp2(s - m_new)
        l_scr[...] = alpha * l_scr[...] + jnp.sum(p, axis=0, keepdims=True)
        m_scr[...] = m_new
        p_cur[...] = p.astype(BF16)
        al_scr[...] = alpha

    def finish(j, p_cur):
        acc_scr[...] = al_scr[...] * acc_scr[...] + _dot(vt_scr[j], p_cur[...])

    sa = (sa_scr, ma_scr)
    sb = (sb_scr, mb_scr)
    put_scores(sa, 0)

    def pair(jj, carry):
        phase(2 * jj, sa, pa_scr, sb, pb_scr, False)
        phase(2 * jj + 1, sb, pb_scr, sa, pa_scr, False)
        return carry

    lax.fori_loop(0, lax.shift_right_logical(i, 1), pair, 0)

    @pl.when((i & 1) == 0)
    def _():
        phase(i, sa, pa_scr, sb, pb_scr, True, prefetch=False)
        finish(i, pa_scr)

    @pl.when((i & 1) == 1)
    def _():
        phase(i - 1, sa, pa_scr, sb, pb_scr, False)
        phase(i, sb, pb_scr, sa, pa_scr, True, prefetch=False)
        finish(i, pb_scr)

    acc = acc_scr[...]
    lp = lamp_ref[...]
    lam = (jnp.exp(jnp.sum(lp[0:1] * lp[1:2], axis=-1, keepdims=True))
           - jnp.exp(jnp.sum(lp[2:3] * lp[3:4], axis=-1, keepdims=True)) + lambda_init)
    rl = 1.0 / l_scr[...]
    ot = acc[:, :tq] * rl[:, :tq] - lam * (acc[:, tq:] * rl[:, tq:])
    ot = ot * (lax.rsqrt(jnp.mean(ot * ot, axis=0, keepdims=True) + EPS) * (1.0 - lambda_init))
    o_ref[...] = (ot.T * subw_ref[...]).astype(BF16)


def _attention(att, lamp, subw, batch, seq, lambda_init, tq=512):
    n = att.shape[0]
    nq = seq // tq
    tk = tq
    body = functools.partial(_attn_body, tq=tq, tk=tk, lambda_init=lambda_init)
    return pl.pallas_call(
        body,
        grid=(batch, ATT_HEADS, nq),
        in_specs=[
            pl.BlockSpec((4, ATT_QK_DIM), lambda b, h, i: (0, 0)),
            pl.BlockSpec((1, ATT_V_DIM), lambda b, h, i: (0, 0)),
            pl.BlockSpec((tq, LANES), lambda b, h, i: (b * nq + i, h)),
            pl.BlockSpec((seq, LANES), lambda b, h, i: (b, ATT_HEADS + h)),
            pl.BlockSpec((seq, LANES), lambda b, h, i: (b, 2 * ATT_HEADS + h)),
        ],
        out_specs=pl.BlockSpec((tq, LANES), lambda b, h, i: (b * nq + i, h)),
        out_shape=jax.ShapeDtypeStruct((n, ATT_WIDTH), BF16),
        scratch_shapes=[
            pltpu.VMEM((seq // tk + 1, LANES, tk), BF16),
            pltpu.VMEM((LANES, 2 * tq), BF16),
            pltpu.VMEM((tk, 2 * tq), F32),
            pltpu.VMEM((tk, 2 * tq), F32),
            pltpu.VMEM((1, 2 * tq), F32),
            pltpu.VMEM((1, 2 * tq), F32),
            pltpu.VMEM((tk, 2 * tq), BF16),
            pltpu.VMEM((tk, 2 * tq), BF16),
            pltpu.VMEM((1, 2 * tq), F32),
            pltpu.VMEM((1, 2 * tq), F32),
            pltpu.VMEM((1, 2 * tq), F32),
            pltpu.VMEM((LANES, 2 * tq), F32),
        ],
        compiler_params=_cparams(("arbitrary", "arbitrary", "arbitrary")),
        name="diff_attention",
    )(lamp, subw, att, att, att)


def _hgrn_body(x_ref, lbl_ref, nw_ref, o_ref, st_scr, b_scr, q_scr, k_scr, v_scr,
               qd_scr, oi_scr, *, layer, rows):
    @pl.when(pl.program_id(1) == 0)
    def _():
        st_scr[...] = jnp.zeros(st_scr.shape, F32)

    lg = lbl_ref[...]
    e = jnp.exp(lg - jnp.max(lg, axis=0, keepdims=True))
    sm = e / jnp.sum(e, axis=0, keepdims=True)
    lb = jnp.zeros((1, REC_WIDTH), F32)
    for i in range(1, layer + 1):
        lb = lb + sm[i:i + 1]

    x = x_ref[...]
    q = x[:, 0:REC_WIDTH]
    fp = x[:, REC_WIDTH:2 * REC_WIDTH]
    iv = x[:, 2 * REC_WIDTH:3 * REC_WIDTH]
    gate = x[:, 3 * REC_WIDTH:4 * REC_WIDTH]
    qf = _silu(q)
    f = lb + (1.0 - lb) * _sigmoid(fp)
    logf = jnp.log2(jnp.maximum(f, F_FLOOR))
    kf = (1.0 - lb) * _sigmoid(-fp)

    b = _cumsum_rows(_chunk_incl(rows, 4), logf)
    def put(ref, val):
        ref[0] = val[:, :LANES]
        ref[1] = val[:, LANES:]

    def rows_at(ref, sl):
        return jnp.concatenate([ref[0, sl, :], ref[1, sl, :]], axis=1)

    def row_rep(ref, r):
        return rows_at(ref, pl.ds(r, HG_CHUNK, stride=0))

    put(b_scr, b)
    put(k_scr, kf)
    put(v_scr, iv)
    q_scr[...] = qf
    qd_scr[...] = qf * jnp.exp2(b)

    same = _same_head((REC_WIDTH, REC_WIDTH))
    ind = jnp.where(same, 1.0, 0.0).astype(BF16)
    trow = lax.broadcasted_iota(jnp.int32, (HG_CHUNK, REC_WIDTH), 0)

    def chunk(c, carry):
        base = c * HG_CHUNK
        sl = pl.ds(base, HG_CHUNK)
        b_c = rows_at(b_scr, sl)
        q_c = q_scr[sl, :]
        b_last = row_rep(b_scr, base + HG_CHUNK - 1)
        kd_c = rows_at(k_scr, sl) * jnp.exp2(b_last - b_c)
        st = st_scr[...]
        o_inter = lax.dot_general(qd_scr[sl, :].astype(BF16), st.astype(BF16),
                                  (((1,), (1,)), ((), ())), preferred_element_type=F32)
        kvt = lax.dot_general(rows_at(v_scr, sl).astype(BF16), kd_c.astype(BF16),
                              (((0,), (0,)), ((), ())), preferred_element_type=F32)
        st_scr[...] = st * jnp.exp2(b_last[0:1]) + jnp.where(same, kvt, 0.0)
        slabs = []
        for s in range(HG_CHUNK):
            rel = jnp.where(trow >= s, b_c - row_rep(b_scr, base + s), MASK_NEG)
            slabs.append((jnp.exp2(rel) * q_c * row_rep(k_scr, base + s)).astype(BF16))
        a = _dot(jnp.concatenate(slabs, axis=0), ind)
        o_intra = a[0:HG_CHUNK] * row_rep(v_scr, base)
        for s in range(1, HG_CHUNK):
            o_intra = o_intra + a[s * HG_CHUNK:(s + 1) * HG_CHUNK] * row_rep(v_scr, base + s)
        oi_scr[sl, :] = o_inter + o_intra
        return carry

    for c in range(rows // HG_CHUNK):
        chunk(c, 0)

    o = oi_scr[...]
    ms = _dot((o * o).astype(BF16), ind) * (1.0 / HEAD_DIM)
    y = o * lax.rsqrt(ms + EPS) * nw_ref[...] * _silu(gate)
    o_ref[...] = y.astype(BF16)


def _hgrn(rest, lb_logits, nw, batch, seq, layer, rows=256):
    n = rest.shape[0]
    nb = seq // rows
    body = functools.partial(_hgrn_body, layer=layer, rows=rows)
    rec = pltpu.VMEM((rows, REC_WIDTH), F32)
    slab = pltpu.VMEM((REC_WIDTH // LANES, rows, LANES), F32)
    return pl.pallas_call(
        body,
        grid=(batch, nb),
        in_specs=[
            pl.BlockSpec((rows, HG_COLS), lambda b, r: (b * nb + r, 0)),
            pl.BlockSpec((DEPTH, REC_WIDTH), lambda b, r: (0, 0)),
            pl.BlockSpec((1, REC_WIDTH), lambda b, r: (0, 0)),
        ],
        out_specs=pl.BlockSpec((rows, REC_WIDTH), lambda b, r: (b * nb + r, 0)),
        out_shape=jax.ShapeDtypeStruct((n, REC_WIDTH), BF16),
        scratch_shapes=[pltpu.VMEM((REC_WIDTH, REC_WIDTH), F32),
                        slab, rec, slab, slab, rec, rec],
        compiler_params=_cparams(("arbitrary", "arbitrary")),
        name="hgrn2",
    )(rest, lb_logits, nw)


def _dn_body(x_ref, a_ref, b_ref, cw_ref, alog_ref, dtb_ref, nw_ref, o_ref,
             carry_scr, xs_scr, ys_scr, s_scr, *, rows):
    nbatch = x_ref.shape[0]
    nchunk = rows // DN_CHUNK
    flat = nbatch * rows

    @pl.when(pl.program_id(0) == 0)
    def _():
        carry_scr[...] = jnp.zeros(carry_scr.shape, F32)
        s_scr[...] = jnp.zeros(s_scr.shape, F32)

    cw = cw_ref[...]
    quarter = rows // ROW_STRIDE
    ys = []
    for bi in range(nbatch):
        for s in range(DN_CONV_CH // LANES):
            lanes = slice(s * LANES, (s + 1) * LANES)
            xs_scr[bi, s, 0:8, :] = carry_scr[bi, s]
            xs_scr[bi, s, 8:, :] = x_ref[bi, :, lanes]
            carry_scr[bi, s] = x_ref[bi, rows - 8:rows, lanes]
            for r in range(ROW_STRIDE):
                acc = cw[0:1, lanes] * xs_scr[bi, s, pl.ds(5 + r, quarter, stride=ROW_STRIDE), :]
                for j in range(1, DN_CONV):
                    acc = acc + cw[j:j + 1, lanes] * xs_scr[bi, s, pl.ds(5 + j + r, quarter,
                                                                         stride=ROW_STRIDE), :]
                ys_scr[bi, s, pl.ds(r, quarter, stride=ROW_STRIDE), :] = _silu(acc)
        ys.append(jnp.concatenate([ys_scr[bi, s] for s in range(DN_CONV_CH // LANES)], axis=1))
    y = jnp.concatenate(ys, axis=0)
    z = x_ref[...].reshape(flat, DN_COLS)[:, DN_CONV_CH:]
    qc = y[:, 0:REC_WIDTH]
    kc = y[:, REC_WIDTH:2 * REC_WIDTH]
    vc = y[:, 2 * REC_WIDTH:]

    same = _same_head((REC_WIDTH, REC_WIDTH))
    ind = jnp.where(same, 1.0, 0.0).astype(BF16)
    qn = qc * lax.rsqrt(_dot((qc * qc).astype(BF16), ind) + EPS) * (HEAD_DIM ** -0.5)
    kn = kc * lax.rsqrt(_dot((kc * kc).astype(BF16), ind) + EPS)

    sp_in = a_ref[...].reshape(flat, REC_WIDTH) + dtb_ref[...]
    softplus = jnp.maximum(sp_in, 0.0) + jnp.log1p(jnp.exp(-jnp.abs(sp_in)))
    g = -jnp.exp(alog_ref[...]) * softplus
    beta = _sigmoid(b_ref[...].reshape(flat, REC_WIDTH))
    kb = kn * beta
    vb = vc * beta

    crow = lax.broadcasted_iota(jnp.int32, (DN_CHUNK, REC_WIDTH), 0)
    slane = lax.broadcasted_iota(jnp.int32, (DN_CHUNK, REC_WIDTH), 1) & (HEAD_DIM - 1)
    diag = crow == slane
    eye = jnp.where(diag, 1.0, 0.0).astype(F32)
    incl = _chunk_incl(DN_CHUNK, 6)

    def bdiag(v):
        v16 = v.astype(BF16)
        return jnp.where(same, jnp.concatenate([v16] * HEADS, axis=0), jnp.zeros((), BF16))

    chunks = [(bi, n) for n in range(nchunk) for bi in range(nbatch)]
    rows_of = {}
    for bi, n in chunks:
        start = bi * rows + n * DN_CHUNK
        rows_of[(bi, n)] = slice(start, start + DN_CHUNK)
    gc, g_row, kt_bd, a_kk, a_qk, xinv, pw, u, w = {}, {}, {}, {}, {}, {}, {}, {}, {}
    for ch in chunks:
        gc[ch] = _cumsum_rows(incl, g[rows_of[ch]])
    for ch in chunks:
        g_row[ch] = jnp.sum(jnp.where(diag, gc[ch], 0.0), axis=0, keepdims=True)
        kt = jnp.concatenate([kn[rows_of[ch]]] * HEADS, axis=0).T
        kt_bd[ch] = jnp.where(same, kt, 0.0)
    for ch in chunks:
        sl = rows_of[ch]
        decay = jnp.exp(jnp.where(slane <= crow, gc[ch] - g_row[ch], MASK_NEG))
        sc = _dot(jnp.concatenate([kb[sl], qn[sl]], axis=0).astype(BF16), kt_bd[ch].astype(BF16))
        a_kk[ch] = jnp.where(slane < crow, sc[:DN_CHUNK] * decay, 0.0)
        a_qk[ch] = sc[DN_CHUNK:] * decay
    for ch in chunks:
        xinv[ch] = eye - a_kk[ch]
        pw[ch] = _dot(a_kk[ch].astype(BF16), bdiag(a_kk[ch]))
    for _ in range(4):
        for ch in chunks:
            r = _dot(jnp.concatenate([xinv[ch], pw[ch]], axis=0).astype(BF16), bdiag(pw[ch]))
            xinv[ch] = xinv[ch] + r[:DN_CHUNK]
            pw[ch] = r[DN_CHUNK:]
    for ch in chunks:
        sl = rows_of[ch]
        xi = (xinv[ch] + _dot(xinv[ch].astype(BF16), bdiag(pw[ch]))).astype(BF16)
        u[ch] = _dot(xi, bdiag(vb[sl]))
        w[ch] = _dot(xi, bdiag(kb[sl] * jnp.exp(gc[ch])))
    outs = {}
    state = [s_scr[bi] for bi in range(nbatch)]
    for n in range(nchunk):
        for bi in range(nbatch):
            ch = (bi, n)
            q_dec = qn[rows_of[ch]] * jnp.exp(gc[ch])
            r1 = _dot(jnp.concatenate([w[ch], q_dec], axis=0).astype(BF16), state[bi].astype(BF16))
            v_new = u[ch] - r1[:DN_CHUNK]
            gl_row = gc[ch][DN_CHUNK - 1:DN_CHUNK]
            kdt_bd = kt_bd[ch] * jnp.exp(gl_row - g_row[ch])
            r2 = _dot(jnp.concatenate([a_qk[ch], kdt_bd], axis=0).astype(BF16), bdiag(v_new))
            outs[ch] = r1[DN_CHUNK:] + r2[:DN_CHUNK]
            state[bi] = state[bi] * jnp.exp(gl_row) + r2[DN_CHUNK:]
    for bi in range(nbatch):
        s_scr[bi] = state[bi]

    o = jnp.concatenate([outs[(bi, n)] for bi in range(nbatch) for n in range(nchunk)], axis=0)
    ms = _dot((o * o).astype(BF16), ind) * (1.0 / HEAD_DIM)
    o = o * lax.rsqrt(ms + EPS) * nw_ref[...] * _silu(z)
    o_ref[...] = o.reshape(nbatch, rows, REC_WIDTH).astype(BF16)


def _deltanet(rest, cw, alog, dtb, nw, batch, seq, rows=256):
    rest3 = rest.reshape(batch, seq, REST_COLS)
    body = functools.partial(_dn_body, rows=rows)
    ab0 = (HG_COLS + DN_COLS) // REC_WIDTH
    out = pl.pallas_call(
        body,
        grid=(seq // rows,),
        in_specs=[
            pl.BlockSpec((batch, rows, DN_COLS), lambda r: (0, r, HG_COLS // DN_COLS)),
            pl.BlockSpec((batch, rows, REC_WIDTH), lambda r: (0, r, ab0)),
            pl.BlockSpec((batch, rows, REC_WIDTH), lambda r: (0, r, ab0 + 1)),
            pl.BlockSpec((DN_CONV, DN_CONV_CH), lambda r: (0, 0)),
            pl.BlockSpec((1, REC_WIDTH), lambda r: (0, 0)),
            pl.BlockSpec((1, REC_WIDTH), lambda r: (0, 0)),
            pl.BlockSpec((1, REC_WIDTH), lambda r: (0, 0)),
        ],
        out_specs=pl.BlockSpec((batch, rows, REC_WIDTH), lambda r: (0, r, 0)),
        out_shape=jax.ShapeDtypeStruct((batch, seq, REC_WIDTH), BF16),
        scratch_shapes=[
            pltpu.VMEM((batch, DN_CONV_CH // LANES, 8, LANES), F32),
            pltpu.VMEM((batch, DN_CONV_CH // LANES, rows + 8, LANES), F32),
            pltpu.VMEM((batch, DN_CONV_CH // LANES, rows, LANES), F32),
            pltpu.VMEM((batch, REC_WIDTH, REC_WIDTH), F32),
        ],
        compiler_params=_cparams(("arbitrary",)),
        name="gated_deltanet",
    )(rest3, rest3, rest3, cw, alog, dtb, nw)
    return out.reshape(batch * seq, REC_WIDTH)


def _mix_ffn_body(att_ref, hg_ref, dn_ref, h_ref, wo_ref, nw_ref, wup_ref, cw_ref, cb_ref, wdn_ref,
                  fw_ref, o_ref, u_scr, act_scr, zg_scr, zv_scr, back_scr,
                  *, tm, tf, blocks_per_seq, final):
    first = pl.program_id(0) % blocks_per_seq == 0

    @pl.when(first)
    def _():
        u_scr[:HALO, :] = jnp.zeros((HALO, D_MODEL), BF16)

    @pl.when(jnp.logical_not(first))
    def _():
        u_scr[:HALO, :] = u_scr[tm:tm + HALO, :]

    h1 = h_ref[...] + _dot(att_ref[...], wo_ref[0:ATT_WIDTH, :])
    h1 = h1 + _dot(hg_ref[...], wo_ref[ATT_WIDTH:ATT_WIDTH + REC_WIDTH, :])
    h1 = h1 + _dot(dn_ref[...], wo_ref[ATT_WIDTH + REC_WIDTH:, :])
    o_ref[...] = h1
    u_scr[HALO:, :] = _rms(h1, nw_ref[...]).astype(BF16)
    u = u_scr[...]

    quarter = tm // ROW_STRIDE

    def park(z_scr, z):
        for s in range(tf // LANES):
            z_scr[s] = z[:, s * LANES:(s + 1) * LANES]

    def rows_back(z_scr, d):
        cols = []
        for s in range(tf // LANES):
            cols.append(jnp.concatenate(
                [z_scr[s, pl.ds(HALO + r - d, quarter, stride=ROW_STRIDE), :] for r in range(ROW_STRIDE)],
                axis=0))
        return jnp.concatenate(cols, axis=1)

    def conv(z_scr, w, b):
        out = b
        for j in range(FFN_CONV):
            out = out + w[j:j + 1] * rows_back(z_scr, FFN_CONV - 1 - j)
        return out

    for j in range(D_FF // tf):
        gs = slice(j * tf, (j + 1) * tf)
        vs = slice(D_FF + j * tf, D_FF + (j + 1) * tf)
        park(zg_scr, _dot(u, wup_ref[:, gs]))
        park(zv_scr, _dot(u, wup_ref[:, vs]))
        gate = conv(zg_scr, cw_ref[:, gs], cb_ref[:, gs])
        val = conv(zv_scr, cw_ref[:, vs], cb_ref[:, vs])
        act_scr[:, gs] = (_silu(gate) * val).astype(BF16)

    act = act_scr[...]
    nstep = 256
    for c0 in range(0, D_MODEL, nstep):
        y = _dot(act, wdn_ref[:, c0:c0 + nstep])
        for s in range(nstep // LANES):
            slab = c0 // LANES + s
            for r in range(ROW_STRIDE):
                back_scr[slab, pl.ds(r, quarter, stride=ROW_STRIDE), :] = (
                    y[r * quarter:(r + 1) * quarter, s * LANES:(s + 1) * LANES])
            lanes = slice(slab * LANES, (slab + 1) * LANES)
            o_ref[:, lanes] = o_ref[:, lanes] + back_scr[slab]
    if final:
        o_ref[...] = _rms(o_ref[...], fw_ref[...])


def _mix_ffn(att_o, hg_o, dn_o, h, wo, nw, wup, cw, cb, wdn, fw, seq, final, tm=512, tf=256):
    n = h.shape[0]
    body = functools.partial(_mix_ffn_body, tm=tm, tf=tf, blocks_per_seq=seq // tm, final=final)
    return pl.pallas_call(
        body,
        grid=(n // tm,),
        in_specs=[
            pl.BlockSpec((tm, ATT_WIDTH), lambda i: (i, 0)),
            pl.BlockSpec((tm, REC_WIDTH), lambda i: (i, 0)),
            pl.BlockSpec((tm, REC_WIDTH), lambda i: (i, 0)),
            pl.BlockSpec((tm, D_MODEL), lambda i: (i, 0)),
            _const_spec((D_MODEL, D_MODEL)),
            _const_spec((1, D_MODEL)),
            _const_spec((D_MODEL, 2 * D_FF)),
            _const_spec((FFN_CONV, 2 * D_FF)),
            _const_spec((1, 2 * D_FF)),
            _const_spec((D_FF, D_MODEL)),
            _const_spec((1, D_MODEL)),
        ],
        out_specs=pl.BlockSpec((tm, D_MODEL), lambda i: (i, 0)),
        out_shape=jax.ShapeDtypeStruct((n, D_MODEL), F32),
        scratch_shapes=[
            pltpu.VMEM((HALO + tm, D_MODEL), BF16),
            pltpu.VMEM((tm, D_FF), BF16),
            pltpu.VMEM((tf // LANES, HALO + tm, LANES), F32),
            pltpu.VMEM((tf // LANES, HALO + tm, LANES), F32),
            pltpu.VMEM((D_MODEL // LANES, tm, LANES), F32),
        ],
        compiler_params=_cparams(("arbitrary",)),
        name="mix_ffn",
    )(att_o, hg_o, dn_o, h, wo, nw, wup, cw, cb, wdn, fw)


def _cast_body(x_ref, o_ref):
    o_ref[...] = x_ref[...].astype(BF16)


CAST_BLOCK_BYTES = 8 * 1024 * 1024


def _cast_layer(w, layer):
    _, r, c = w.shape
    rb = r
    while rb * c * 4 > CAST_BLOCK_BYTES and rb % 32 == 0:
        rb //= 2
    return pl.pallas_call(
        _cast_body,
        grid=(r // rb,),
        in_specs=[pl.BlockSpec((None, rb, c), lambda i: (layer, i, 0))],
        out_specs=pl.BlockSpec((rb, c), lambda i: (i, 0)),
        out_shape=jax.ShapeDtypeStruct((r, c), BF16),
        compiler_params=_cparams(("arbitrary",)),
        name="cast_weight",
    )(w)


W_IN_STEP = 512


def _prep_w_in_body(x_ref, o_ref):
    j = pl.program_id(0)
    last = D_IN_WIDE // W_IN_STEP - 1

    @pl.when(j == 0)
    def _():
        o_ref[...] = (x_ref[...] * (ATT_QK_DIM ** -0.5)).T.astype(BF16)

    @pl.when(jnp.logical_and(j > 0, j < last))
    def _():
        o_ref[...] = x_ref[...].T.astype(BF16)

    @pl.when(j == last)
    def _():
        rep = [jnp.broadcast_to(x_ref[r:r + 1, :], (HEAD_DIM, D_MODEL)) for r in range(2 * HEADS)]
        o_ref[...] = jnp.concatenate(rep, axis=0).T.astype(BF16)


def _prep_w_in(w_in, layer):
    assert D_IN - 2 * HEADS == D_IN_WIDE - W_IN_STEP and AB_COLS == W_IN_STEP
    w_t = jnp.swapaxes(w_in, 1, 2)
    return pl.pallas_call(
        _prep_w_in_body,
        grid=(D_IN_WIDE // W_IN_STEP,),
        in_specs=[pl.BlockSpec((None, W_IN_STEP, D_MODEL), lambda j: (layer, j, 0))],
        out_specs=pl.BlockSpec((D_MODEL, W_IN_STEP), lambda j: (0, j)),
        out_shape=jax.ShapeDtypeStruct((D_MODEL, D_IN_WIDE), BF16),
        compiler_params=_cparams(("arbitrary",)),
        name="prep_w_in",
    )(w_t)


def kernel(x, attn_norm_w, w_in, diff_lambda, diff_subln_w, hgrn_lb_logits, hgrn_norm_w, dn_conv_w,
           dn_A_log, dn_dt_bias, dn_norm_w, w_out, ffn_norm_w, ffn_w_up, ffn_conv_w, ffn_conv_b,
           ffn_w_down, final_norm_w):
    batch, seq, _ = x.shape
    h = x.reshape(batch * seq, D_MODEL)
    dn_alog = jnp.repeat(dn_A_log, HEAD_DIM, axis=-1)
    dn_dtb = jnp.repeat(dn_dt_bias, HEAD_DIM, axis=-1)
    hg_nw = jnp.tile(hgrn_norm_w, (1, HEADS))
    dn_nw = jnp.tile(dn_norm_w, (1, HEADS))
    for l in range(DEPTH):
        lambda_init = 0.8 - 0.6 * math.exp(-0.3 * l)
        att, rest = _inproj(h, attn_norm_w[l][None], _prep_w_in(w_in, l))
        att_o = _attention(att, diff_lambda[l], diff_subln_w[l][None], batch, seq, lambda_init)
        hg_o = _hgrn(rest, hgrn_lb_logits, hg_nw[l][None], batch, seq, l)
        dn_o = _deltanet(rest, dn_conv_w[l], dn_alog[l][None], dn_dtb[l][None], dn_nw[l][None],
                         batch, seq)
        h = _mix_ffn(att_o, hg_o, dn_o, h, _cast_layer(w_out, l), ffn_norm_w[l][None],
                     _cast_layer(ffn_w_up, l), ffn_conv_w[l], ffn_conv_b[l][None],
                     _cast_layer(ffn_w_down, l), final_norm_w[None], seq, final=(l == DEPTH - 1))
    return h.reshape(batch, seq, D_MODEL)
```

```python
import functools
import math

import jax
import jax.numpy as jnp
from jax import lax
from jax.experimental import pallas as pl
from jax.experimental.pallas import tpu as pltpu

F32 = jnp.float32
BF16 = jnp.bfloat16

D_MODEL = 1024
DEPTH = 2
ATT_QK_DIM = 64
ATT_V_DIM = 128
ATT_HEADS = 4
ATT_WIDTH = 512
HEADS = 4
HEAD_DIM = 64
REC_WIDTH = HEADS * HEAD_DIM
DN_CONV = 4
DN_CONV_CH = 3 * REC_WIDTH
FFN_CONV = 3
D_FF = 2816
HG_CHUNK = 16
DN_CHUNK = 64
EPS = 1e-6
MASK_NEG = -1e30
F_FLOOR = 1e-30
LOG2E = math.log2(math.e)

ATT_COLS = 3 * ATT_WIDTH
HG_COLS = 4 * REC_WIDTH
DN_COLS = 4 * REC_WIDTH
AB_COLS = 2 * REC_WIDTH
REST_COLS = HG_COLS + DN_COLS + AB_COLS
D_IN = ATT_COLS + HG_COLS + DN_COLS + 2 * HEADS
D_IN_WIDE = ATT_COLS + REST_COLS

VMEM_LIMIT = 56 * 1024 * 1024
LANES = 128
HALO = 16
ROW_STRIDE = 4

def _cparams(sem):
    return pltpu.CompilerParams(dimension_semantics=sem, vmem_limit_bytes=VMEM_LIMIT)


def _const_spec(shape):
    return pl.BlockSpec(shape, lambda *_: (0,) * len(shape), pipeline_mode=pl.Buffered(1))


def _rms(x, w):
    return x * lax.rsqrt(jnp.mean(x * x, axis=-1, keepdims=True) + EPS) * w


def _sigmoid(x):
    return 1.0 / (1.0 + jnp.exp(-x))


def _silu(x):
    return x * _sigmoid(x)


def _dot(a, b):
    return jnp.dot(a, b, preferred_element_type=F32)


def _head_of(idx):
    return lax.shift_right_logical(idx, 6)


def _same_head(shape):
    r = lax.broadcasted_iota(jnp.int32, shape, 0)
    c = lax.broadcasted_iota(jnp.int32, shape, 1)
    return _head_of(r) == _head_of(c)


def _cumsum_rows(incl, x):
    hi = x.astype(BF16)
    r1 = x - hi.astype(F32)
    mid = r1.astype(BF16)
    lo = (r1 - mid.astype(F32)).astype(BF16)
    return _dot(incl, hi) + _dot(incl, mid) + _dot(incl, lo)


def _chunk_incl(rows, chunk_shift):
    r = lax.broadcasted_iota(jnp.int32, (rows, rows), 0)
    c = lax.broadcasted_iota(jnp.int32, (rows, rows), 1)
    same = lax.shift_right_logical(r, chunk_shift) == lax.shift_right_logical(c, chunk_shift)
    return jnp.where(same & (c <= r), 1.0, 0.0).astype(BF16)


def _inproj_body(x_ref, nw_ref, w_ref, att_ref, rest_ref):
    u = _rms(x_ref[...], nw_ref[...]).astype(BF16)
    step = 512
    for c0 in range(0, ATT_COLS, step):
        att_ref[:, c0:c0 + step] = _dot(u, w_ref[:, c0:c0 + step]).astype(BF16)
    for c0 in range(0, REST_COLS, step):
        rest_ref[:, c0:c0 + step] = _dot(u, w_ref[:, ATT_COLS + c0:ATT_COLS + c0 + step])


def _inproj(h, nw, w, tm=1024):
    n = h.shape[0]
    return pl.pallas_call(
        _inproj_body,
        grid=(n // tm,),
        in_specs=[
            pl.BlockSpec((tm, D_MODEL), lambda i: (i, 0)),
            _const_spec((1, D_MODEL)),
            _const_spec((D_MODEL, D_IN_WIDE)),
        ],
        out_specs=[
            pl.BlockSpec((tm, ATT_COLS), lambda i: (i, 0)),
            pl.BlockSpec((tm, REST_COLS), lambda i: (i, 0)),
        ],
        out_shape=[
            jax.ShapeDtypeStruct((n, ATT_COLS), BF16),
            jax.ShapeDtypeStruct((n, REST_COLS), F32),
        ],
        compiler_params=_cparams(("arbitrary",)),
        name="inproj",
    )(h, nw, w)


def _attn_body(lamp_ref, subw_ref, q_ref, k_ref, v_ref, o_ref,
               vt_scr, qt_scr, sa_scr, sb_scr, ma_scr, mb_scr, pa_scr, pb_scr, m_scr, l_scr, al_scr,
               acc_scr, *, tq, tk, lambda_init):
    i = pl.program_id(2)
    nblk = vt_scr.shape[0] - 1

    @pl.when(i == 0)
    def _():
        for c in range(nblk):
            vt_scr[c] = v_ref[c * tk:(c + 1) * tk, :].astype(F32).T.astype(BF16)
        vt_scr[nblk] = jnp.zeros(vt_scr.shape[1:], BF16)
        pb_scr[...] = jnp.zeros(pb_scr.shape, BF16)

    qt = q_ref[...].astype(F32).T * LOG2E
    first = lax.broadcasted_iota(jnp.int32, qt.shape, 0) < ATT_QK_DIM
    qt_scr[:, :tq] = jnp.where(first, qt, 0.0).astype(BF16)
    qt_scr[:, tq:] = jnp.where(first, 0.0, qt).astype(BF16)
    m_scr[...] = jnp.full(m_scr.shape, MASK_NEG, F32)
    l_scr[...] = jnp.zeros(l_scr.shape, F32)
    al_scr[...] = jnp.ones(al_scr.shape, F32)
    acc_scr[...] = jnp.zeros(acc_scr.shape, F32)

    def put_scores(dst, j):
        kj = k_ref[pl.ds(pl.multiple_of(j * tk, tk), tk), :]
        s = _dot(kj, qt_scr[...])
        dst[0][...] = s
        dst[1][...] = jnp.max(s, axis=0, keepdims=True)

    def phase(j, s_cur, p_cur, s_nxt, p_prv, masked, prefetch=True):
        if prefetch:
            put_scores(s_nxt, j + 1)
        pv = _dot(vt_scr[jnp.where(j == 0, nblk, j - 1)], p_prv[...])
        acc_scr[...] = al_scr[...] * acc_scr[...] + pv
        s = s_cur[0][...]
        if masked:
            kpos = lax.broadcasted_iota(jnp.int32, s.shape, 0) + (j * tk - i * tq)
            qpos = lax.broadcasted_iota(jnp.int32, s.shape, 1)
            qpos = jnp.where(qpos >= tq, qpos - tq, qpos)
            s = jnp.where(kpos <= qpos, s, MASK_NEG)
            m_tile = jnp.max(s, axis=0, keepdims=True)
        else:
            m_tile = s_cur[1][...]
        m_old = m_scr[...]
        m_new = jnp.maximum(m_old, m_tile)
        alpha = jnp.exp2(m_old - m_new)
        p = jnp.exp2(s - m_new)
        l_scr[...] = alpha * l_scr[...] + jnp.sum(p, axis=0, keepdims=True)
        m_scr[...] = m_new
        p_cur[...] = p.astype(BF16)
        al_scr[...] = alpha

    def finish(j, p_cur):
        acc_scr[...] = al_scr[...] * acc_scr[...] + _dot(vt_scr[j], p_cur[...])

    sa = (sa_scr, ma_scr)
    sb = (sb_scr, mb_scr)
    put_scores(sa, 0)

    def pair(jj, carry):
        phase(2 * jj, sa, pa_scr, sb, pb_scr, False)
        phase(2 * jj + 1, sb, pb_scr, sa, pa_scr, False)
        return carry

    lax.fori_loop(0, lax.shift_right_logical(i, 1), pair, 0)

    @pl.when((i & 1) == 0)
    def _():
        phase(i, sa, pa_scr, sb, pb_scr, True, prefetch=False)
        finish(i, pa_scr)

    @pl.when((i & 1) == 1)
    def _():
        phase(i - 1, sa, pa_scr, sb, pb_scr, False)
        phase(i, sb, pb_scr, sa, pa_scr, True, prefetch=False)
        finish(i, pb_scr)

    acc = acc_scr[...]
    lp = lamp_ref[...]
    lam = (jnp.exp(jnp.sum(lp[0:1] * lp[1:2], axis=-1, keepdims=True))
           - jnp.exp(jnp.sum(lp[2:3] * lp[3:4], axis=-1, keepdims=True)) + lambda_init)
    rl = 1.0 / l_scr[...]
    ot = acc[:, :tq] * rl[:, :tq] - lam * (acc[:, tq:] * rl[:, tq:])
    ot = ot * (lax.rsqrt(jnp.mean(ot * ot, axis=0, keepdims=True) + EPS) * (1.0 - lambda_init))
    o_ref[...] = (ot.T * subw_ref[...]).astype(BF16)


def _attention(att, lamp, subw, batch, seq, lambda_init, tq=512):
    n = att.shape[0]
    nq = seq // tq
    tk = tq
    body = functools.partial(_attn_body, tq=tq, tk=tk, lambda_init=lambda_init)
    return pl.pallas_call(
        body,
        grid=(batch, ATT_HEADS, nq),
        in_specs=[
            pl.BlockSpec((4, ATT_QK_DIM), lambda b, h, i: (0, 0)),
            pl.BlockSpec((1, ATT_V_DIM), lambda b, h, i: (0, 0)),
            pl.BlockSpec((tq, LANES), lambda b, h, i: (b * nq + i, h)),
            pl.BlockSpec((seq, LANES), lambda b, h, i: (b, ATT_HEADS + h)),
            pl.BlockSpec((seq, LANES), lambda b, h, i: (b, 2 * ATT_HEADS + h)),
        ],
        out_specs=pl.BlockSpec((tq, LANES), lambda b, h, i: (b * nq + i, h)),
        out_shape=jax.ShapeDtypeStruct((n, ATT_WIDTH), BF16),
        scratch_shapes=[
            pltpu.VMEM((seq // tk + 1, LANES, tk), BF16),
            pltpu.VMEM((LANES, 2 * tq), BF16),
            pltpu.VMEM((tk, 2 * tq), F32),
            pltpu.VMEM((tk, 2 * tq), F32),
            pltpu.VMEM((1, 2 * tq), F32),
            pltpu.VMEM((1, 2 * tq), F32),
            pltpu.VMEM((tk, 2 * tq), BF16),
            pltpu.VMEM((tk, 2 * tq), BF16),
            pltpu.VMEM((1, 2 * tq), F32),
            pltpu.VMEM((1, 2 * tq), F32),
            pltpu.VMEM((1, 2 * tq), F32),
            pltpu.VMEM((LANES, 2 * tq), F32),
        ],
        compiler_params=_cparams(("arbitrary", "arbitrary", "arbitrary")),
        name="diff_attention",
    )(lamp, subw, att, att, att)


def _hgrn_body(x_ref, lbl_ref, nw_ref, o_ref, st_scr, b_scr, q_scr, k_scr, v_scr,
               qd_scr, oi_scr, *, layer, rows):
    @pl.when(pl.program_id(1) == 0)
    def _():
        st_scr[...] = jnp.zeros(st_scr.shape, F32)

    lg = lbl_ref[...]
    e = jnp.exp(lg - jnp.max(lg, axis=0, keepdims=True))
    sm = e / jnp.sum(e, axis=0, keepdims=True)
    lb = jnp.zeros((1, REC_WIDTH), F32)
    for i in range(1, layer + 1):
        lb = lb + sm[i:i + 1]

    x = x_ref[...]
    q = x[:, 0:REC_WIDTH]
    fp = x[:, REC_WIDTH:2 * REC_WIDTH]
    iv = x[:, 2 * REC_WIDTH:3 * REC_WIDTH]
    gate = x[:, 3 * REC_WIDTH:4 * REC_WIDTH]
    qf = _silu(q)
    f = lb + (1.0 - lb) * _sigmoid(fp)
    logf = jnp.log2(jnp.maximum(f, F_FLOOR))
    kf = (1.0 - lb) * _sigmoid(-fp)

    b = _cumsum_rows(_chunk_incl(rows, 4), logf)
    def put(ref, val):
        ref[0] = val[:, :LANES]
        ref[1] = val[:, LANES:]

    def rows_at(ref, sl):
        return jnp.concatenate([ref[0, sl, :], ref[1, sl, :]], axis=1)

    def row_rep(ref, r):
        return rows_at(ref, pl.ds(r, HG_CHUNK, stride=0))

    put(b_scr, b)
    put(k_scr, kf)
    put(v_scr, iv)
    q_scr[...] = qf
    qd_scr[...] = qf * jnp.exp2(b)

    same = _same_head((REC_WIDTH, REC_WIDTH))
    ind = jnp.where(same, 1.0, 0.0).astype(BF16)
    trow = lax.broadcasted_iota(jnp.int32, (HG_CHUNK, REC_WIDTH), 0)

    def chunk(c, carry):
        base = c * HG_CHUNK
        sl = pl.ds(base, HG_CHUNK)
        b_c = rows_at(b_scr, sl)
        q_c = q_scr[sl, :]
        b_last = row_rep(b_scr, base + HG_CHUNK - 1)
        kd_c = rows_at(k_scr, sl) * jnp.exp2(b_last - b_c)
        st = st_scr[...]
        o_inter = lax.dot_general(qd_scr[sl, :].astype(BF16), st.astype(BF16),
                                  (((1,), (1,)), ((), ())), preferred_element_type=F32)
        kvt = lax.dot_general(rows_at(v_scr, sl).astype(BF16), kd_c.astype(BF16),
                              (((0,), (0,)), ((), ())), preferred_element_type=F32)
        st_scr[...] = st * jnp.exp2(b_last[0:1]) + jnp.where(same, kvt, 0.0)
        slabs = []
        for s in range(HG_CHUNK):
            rel = jnp.where(trow >= s, b_c - row_rep(b_scr, base + s), MASK_NEG)
            slabs.append((jnp.exp2(rel) * q_c * row_rep(k_scr, base + s)).astype(BF16))
        a = _dot(jnp.concatenate(slabs, axis=0), ind)
        o_intra = a[0:HG_CHUNK] * row_rep(v_scr, base)
        for s in range(1, HG_CHUNK):
            o_intra = o_intra + a[s * HG_CHUNK:(s + 1) * HG_CHUNK] * row_rep(v_scr, base + s)
        oi_scr[sl, :] = o_inter + o_intra
        return carry

    for c in range(rows // HG_CHUNK):
        chunk(c, 0)

    o = oi_scr[...]
    ms = _dot((o * o).astype(BF16), ind) * (1.0 / HEAD_DIM)
    y = o * lax.rsqrt(ms + EPS) * nw_ref[...] * _silu(gate)
    o_ref[...] = y.astype(BF16)


def _hgrn(rest, lb_logits, nw, batch, seq, layer, rows=256):
    n = rest.shape[0]
    nb = seq // rows
    body = functools.partial(_hgrn_body, layer=layer, rows=rows)
    rec = pltpu.VMEM((rows, REC_WIDTH), F32)
    slab = pltpu.VMEM((REC_WIDTH // LANES, rows, LANES), F32)
    return pl.pallas_call(
        body,
        grid=(batch, nb),
        in_specs=[
            pl.BlockSpec((rows, HG_COLS), lambda b, r: (b * nb + r, 0)),
            pl.BlockSpec((DEPTH, REC_WIDTH), lambda b, r: (0, 0)),
            pl.BlockSpec((1, REC_WIDTH), lambda b, r: (0, 0)),
        ],
        out_specs=pl.BlockSpec((rows, REC_WIDTH), lambda b, r: (b * nb + r, 0)),
        out_shape=jax.ShapeDtypeStruct((n, REC_WIDTH), BF16),
        scratch_shapes=[pltpu.VMEM((REC_WIDTH, REC_WIDTH), F32),
                        slab, rec, slab, slab, rec, rec],
        compiler_params=_cparams(("arbitrary", "arbitrary")),
        name="hgrn2",
    )(rest, lb_logits, nw)


def _dn_body(x_ref, a_ref, b_ref, cw_ref, alog_ref, dtb_ref, nw_ref, o_ref,
             carry_scr, xs_scr, ys_scr, s_scr, *, rows):
    nbatch = x_ref.shape[0]
    nchunk = rows // DN_CHUNK
    flat = nbatch * rows

    @pl.when(pl.program_id(0) == 0)
    def _():
        carry_scr[...] = jnp.zeros(carry_scr.shape, F32)
        s_scr[...] = jnp.zeros(s_scr.shape, F32)

    cw = cw_ref[...]
    quarter = rows // ROW_STRIDE
    ys = []
    for bi in range(nbatch):
        for s in range(DN_CONV_CH // LANES):
            lanes = slice(s * LANES, (s + 1) * LANES)
            xs_scr[bi, s, 0:8, :] = carry_scr[bi, s]
            xs_scr[bi, s, 8:, :] = x_ref[bi, :, lanes]
            carry_scr[bi, s] = x_ref[bi, rows - 8:rows, lanes]
            for r in range(ROW_STRIDE):
                acc = cw[0:1, lanes] * xs_scr[bi, s, pl.ds(5 + r, quarter, stride=ROW_STRIDE), :]
                for j in range(1, DN_CONV):
                    acc = acc + cw[j:j + 1, lanes] * xs_scr[bi, s, pl.ds(5 + j + r, quarter,
                                                                         stride=ROW_STRIDE), :]
                ys_scr[bi, s, pl.ds(r, quarter, stride=ROW_STRIDE), :] = _silu(acc)
        ys.append(jnp.concatenate([ys_scr[bi, s] for s in range(DN_CONV_CH // LANES)], axis=1))
    y = jnp.concatenate(ys, axis=0)
    z = x_ref[...].reshape(flat, DN_COLS)[:, DN_CONV_CH:]
    qc = y[:, 0:REC_WIDTH]
    kc = y[:, REC_WIDTH:2 * REC_WIDTH]
    vc = y[:, 2 * REC_WIDTH:]

    same = _same_head((REC_WIDTH, REC_WIDTH))
    ind = jnp.where(same, 1.0, 0.0).astype(BF16)
    qn = qc * lax.rsqrt(_dot((qc * qc).astype(BF16), ind) + EPS) * (HEAD_DIM ** -0.5)
    kn = kc * lax.rsqrt(_dot((kc * kc).astype(BF16), ind) + EPS)

    sp_in = a_ref[...].reshape(flat, REC_WIDTH) + dtb_ref[...]
    softplus = jnp.maximum(sp_in, 0.0) + jnp.log1p(jnp.exp(-jnp.abs(sp_in)))
    g = -jnp.exp(alog_ref[...]) * softplus
    beta = _sigmoid(b_ref[...].reshape(flat, REC_WIDTH))
    kb = kn * beta
    vb = vc * beta

    crow = lax.broadcasted_iota(jnp.int32, (DN_CHUNK, REC_WIDTH), 0)
    slane = lax.broadcasted_iota(jnp.int32, (DN_CHUNK, REC_WIDTH), 1) & (HEAD_DIM - 1)
    diag = crow == slane
    eye = jnp.where(diag, 1.0, 0.0).astype(F32)
    incl = _chunk_incl(DN_CHUNK, 6)

    def bdiag(v):
        return jnp.where(same, jnp.concatenate([v] * HEADS, axis=0), 0.0).astype(BF16)

    chunks = [(bi, n) for n in range(nchunk) for bi in range(nbatch)]
    rows_of = {}
    for bi, n in chunks:
        start = bi * rows + n * DN_CHUNK
        rows_of[(bi, n)] = slice(start, start + DN_CHUNK)
    gc, g_row, kt_bd, a_kk, a_qk, xinv, pw, u, w = {}, {}, {}, {}, {}, {}, {}, {}, {}
    for ch in chunks:
        gc[ch] = _cumsum_rows(incl, g[rows_of[ch]])
    for ch in chunks:
        g_row[ch] = jnp.sum(jnp.where(diag, gc[ch], 0.0), axis=0, keepdims=True)
        kt = jnp.concatenate([kn[rows_of[ch]]] * HEADS, axis=0).T
        kt_bd[ch] = jnp.where(same, kt, 0.0)
    for ch in chunks:
        sl = rows_of[ch]
        decay = jnp.exp(jnp.where(slane <= crow, gc[ch] - g_row[ch], MASK_NEG))
        sc = _dot(jnp.concatenate([kb[sl], qn[sl]], axis=0).astype(BF16), kt_bd[ch].astype(BF16))
        a_kk[ch] = jnp.where(slane < crow, sc[:DN_CHUNK] * decay, 0.0)
        a_qk[ch] = sc[DN_CHUNK:] * decay
    for ch in chunks:
        xinv[ch] = eye - a_kk[ch]
        pw[ch] = _dot(a_kk[ch].astype(BF16), bdiag(a_kk[ch]))
    for _ in range(4):
        for ch in chunks:
            r = _dot(jnp.concatenate([xinv[ch], pw[ch]], axis=0).astype(BF16), bdiag(pw[ch]))
            xinv[ch] = xinv[ch] + r[:DN_CHUNK]
            pw[ch] = r[DN_CHUNK:]
    for ch in chunks:
        sl = rows_of[ch]
        xi = (xinv[ch] + _dot(xinv[ch].astype(BF16), bdiag(pw[ch]))).astype(BF16)
        u[ch] = _dot(xi, bdiag(vb[sl]))
        w[ch] = _dot(xi, bdiag(kb[sl] * jnp.exp(gc[ch])))
    outs = {}
    state = [s_scr[bi] for bi in range(nbatch)]
    for n in range(nchunk):
        for bi in range(nbatch):
            ch = (bi, n)
            q_dec = qn[rows_of[ch]] * jnp.exp(gc[ch])
            r1 = _dot(jnp.concatenate([w[ch], q_dec], axis=0).astype(BF16), state[bi].astype(BF16))
            v_new = u[ch] - r1[:DN_CHUNK]
            gl_row = gc[ch][DN_CHUNK - 1:DN_CHUNK]
            kdt_bd = kt_bd[ch] * jnp.exp(gl_row - g_row[ch])
            r2 = _dot(jnp.concatenate([a_qk[ch], kdt_bd], axis=0).astype(BF16), bdiag(v_new))
            outs[ch] = r1[DN_CHUNK:] + r2[:DN_CHUNK]
            state[bi] = state[bi] * jnp.exp(gl_row) + r2[DN_CHUNK:]
    for bi in range(nbatch):
        s_scr[bi] = state[bi]

    o = jnp.concatenate([outs[(bi, n)] for bi in range(nbatch) for n in range(nchunk)], axis=0)
    ms = _dot((o * o).astype(BF16), ind) * (1.0 / HEAD_DIM)
    o = o * lax.rsqrt(ms + EPS) * nw_ref[...] * _silu(z)
    o_ref[...] = o.reshape(nbatch, rows, REC_WIDTH).astype(BF16)


def _deltanet(rest, cw, alog, dtb, nw, batch, seq, rows=256):
    rest3 = rest.reshape(batch, seq, REST_COLS)
    body = functools.partial(_dn_body, rows=rows)
    ab0 = (HG_COLS + DN_COLS) // REC_WIDTH
    out = pl.pallas_call(
        body,
        grid=(seq // rows,),
        in_specs=[
            pl.BlockSpec((batch, rows, DN_COLS), lambda r: (0, r, HG_COLS // DN_COLS)),
            pl.BlockSpec((batch, rows, REC_WIDTH), lambda r: (0, r, ab0)),
            pl.BlockSpec((batch, rows, REC_WIDTH), lambda r: (0, r, ab0 + 1)),
            pl.BlockSpec((DN_CONV, DN_CONV_CH), lambda r: (0, 0)),
            pl.BlockSpec((1, REC_WIDTH), lambda r: (0, 0)),
            pl.BlockSpec((1, REC_WIDTH), lambda r: (0, 0)),
            pl.BlockSpec((1, REC_WIDTH), lambda r: (0, 0)),
        ],
        out_specs=pl.BlockSpec((batch, rows, REC_WIDTH), lambda r: (0, r, 0)),
        out_shape=jax.ShapeDtypeStruct((batch, seq, REC_WIDTH), BF16),
        scratch_shapes=[
            pltpu.VMEM((batch, DN_CONV_CH // LANES, 8, LANES), F32),
            pltpu.VMEM((batch, DN_CONV_CH // LANES, rows + 8, LANES), F32),
            pltpu.VMEM((batch, DN_CONV_CH // LANES, rows, LANES), F32),
            pltpu.VMEM((batch, REC_WIDTH, REC_WIDTH), F32),
        ],
        compiler_params=_cparams(("arbitrary",)),
        name="gated_deltanet",
    )(rest3, rest3, rest3, cw, alog, dtb, nw)
    return out.reshape(batch * seq, REC_WIDTH)


def _mix_ffn_body(att_ref, hg_ref, dn_ref, h_ref, wo_ref, nw_ref, wup_ref, cw_ref, cb_ref, wdn_ref,
                  fw_ref, o_ref, u_scr, act_scr, zg_scr, zv_scr, back_scr,
                  *, tm, tf, blocks_per_seq, final):
    first = pl.program_id(0) % blocks_per_seq == 0

    @pl.when(first)
    def _():
        u_scr[:HALO, :] = jnp.zeros((HALO, D_MODEL), BF16)

    @pl.when(jnp.logical_not(first))
    def _():
        u_scr[:HALO, :] = u_scr[tm:tm + HALO, :]

    h1 = h_ref[...] + _dot(att_ref[...], wo_ref[0:ATT_WIDTH, :])
    h1 = h1 + _dot(hg_ref[...], wo_ref[ATT_WIDTH:ATT_WIDTH + REC_WIDTH, :])
    h1 = h1 + _dot(dn_ref[...], wo_ref[ATT_WIDTH + REC_WIDTH:, :])
    o_ref[...] = h1
    u_scr[HALO:, :] = _rms(h1, nw_ref[...]).astype(BF16)
    u = u_scr[...]

    quarter = tm // ROW_STRIDE

    def park(z_scr, z):
        for s in range(tf // LANES):
            z_scr[s] = z[:, s * LANES:(s + 1) * LANES]

    def rows_back(z_scr, d):
        cols = []
        for s in range(tf // LANES):
            cols.append(jnp.concatenate(
                [z_scr[s, pl.ds(HALO + r - d, quarter, stride=ROW_STRIDE), :] for r in range(ROW_STRIDE)],
                axis=0))
        return jnp.concatenate(cols, axis=1)

    def conv(z_scr, w, b):
        out = b
        for j in range(FFN_CONV):
            out = out + w[j:j + 1] * rows_back(z_scr, FFN_CONV - 1 - j)
        return out

    for j in range(D_FF // tf):
        gs = slice(j * tf, (j + 1) * tf)
        vs = slice(D_FF + j * tf, D_FF + (j + 1) * tf)
        park(zg_scr, _dot(u, wup_ref[:, gs]))
        park(zv_scr, _dot(u, wup_ref[:, vs]))
        gate = conv(zg_scr, cw_ref[:, gs], cb_ref[:, gs])
        val = conv(zv_scr, cw_ref[:, vs], cb_ref[:, vs])
        act_scr[:, gs] = (_silu(gate) * val).astype(BF16)

    act = act_scr[...]
    nstep = 256
    for c0 in range(0, D_MODEL, nstep):
        y = _dot(act, wdn_ref[:, c0:c0 + nstep])
        for s in range(nstep // LANES):
            slab = c0 // LANES + s
            for r in range(ROW_STRIDE):
                back_scr[slab, pl.ds(r, quarter, stride=ROW_STRIDE), :] = (
                    y[r * quarter:(r + 1) * quarter, s * LANES:(s + 1) * LANES])
            lanes = slice(slab * LANES, (slab + 1) * LANES)
            o_ref[:, lanes] = o_ref[:, lanes] + back_scr[slab]
    if final:
        o_ref[...] = _rms(o_ref[...], fw_ref[...])


def _mix_ffn(att_o, hg_o, dn_o, h, wo, nw, wup, cw, cb, wdn, fw, seq, final, tm=512, tf=256):
    n = h.shape[0]
    body = functools.partial(_mix_ffn_body, tm=tm, tf=tf, blocks_per_seq=seq // tm, final=final)
    return pl.pallas_call(
        body,
        grid=(n // tm,),
        in_specs=[
            pl.BlockSpec((tm, ATT_WIDTH), lambda i: (i, 0)),
            pl.BlockSpec((tm, REC_WIDTH), lambda i: (i, 0)),
            pl.BlockSpec((tm, REC_WIDTH), lambda i: (i, 0)),
            pl.BlockSpec((tm, D_MODEL), lambda i: (i, 0)),
            _const_spec((D_MODEL, D_MODEL)),
            _const_spec((1, D_MODEL)),
            _const_spec((D_MODEL, 2 * D_FF)),
            _const_spec((FFN_CONV, 2 * D_FF)),
            _const_spec((1, 2 * D_FF)),
            _const_spec((D_FF, D_MODEL)),
            _const_spec((1, D_MODEL)),
        ],
        out_specs=pl.BlockSpec((tm, D_MODEL), lambda i: (i, 0)),
        out_shape=jax.ShapeDtypeStruct((n, D_MODEL), F32),
        scratch_shapes=[
            pltpu.VMEM((HALO + tm, D_MODEL), BF16),
            pltpu.VMEM((tm, D_FF), BF16),
            pltpu.VMEM((tf // LANES, HALO + tm, LANES), F32),
            pltpu.VMEM((tf // LANES, HALO + tm, LANES), F32),
            pltpu.VMEM((D_MODEL // LANES, tm, LANES), F32),
        ],
        compiler_params=_cparams(("arbitrary",)),
        name="mix_ffn",
    )(att_o, hg_o, dn_o, h, wo, nw, wup, cw, cb, wdn, fw)


def _cast_body(x_ref, o_ref):
    o_ref[...] = x_ref[...].astype(BF16)


CAST_BLOCK_BYTES = 8 * 1024 * 1024


def _cast_layer(w, layer):
    _, r, c = w.shape
    rb = r
    while rb * c * 4 > CAST_BLOCK_BYTES and rb % 32 == 0:
        rb //= 2
    return pl.pallas_call(
        _cast_body,
        grid=(r // rb,),
        in_specs=[pl.BlockSpec((None, rb, c), lambda i: (layer, i, 0))],
        out_specs=pl.BlockSpec((rb, c), lambda i: (i, 0)),
        out_shape=jax.ShapeDtypeStruct((r, c), BF16),
        compiler_params=_cparams(("arbitrary",)),
        name="cast_weight",
    )(w)


W_IN_STEP = 512


def _prep_w_in_body(x_ref, o_ref):
    j = pl.program_id(0)
    last = D_IN_WIDE // W_IN_STEP - 1

    @pl.when(j == 0)
    def _():
        o_ref[...] = (x_ref[...] * (ATT_QK_DIM ** -0.5)).T.astype(BF16)

    @pl.when(jnp.logical_and(j > 0, j < last))
    def _():
        o_ref[...] = x_ref[...].T.astype(BF16)

    @pl.when(j == last)
    def _():
        rep = [jnp.broadcast_to(x_ref[r:r + 1, :], (HEAD_DIM, D_MODEL)) for r in range(2 * HEADS)]
        o_ref[...] = jnp.concatenate(rep, axis=0).T.astype(BF16)


def _prep_w_in(w_in, layer):
    assert D_IN - 2 * HEADS == D_IN_WIDE - W_IN_STEP and AB_COLS == W_IN_STEP
    w_t = jnp.swapaxes(w_in, 1, 2)
    return pl.pallas_call(
        _prep_w_in_body,
        grid=(D_IN_WIDE // W_IN_STEP,),
        in_specs=[pl.BlockSpec((None, W_IN_STEP, D_MODEL), lambda j: (layer, j, 0))],
        out_specs=pl.BlockSpec((D_MODEL, W_IN_STEP), lambda j: (0, j)),
        out_shape=jax.ShapeDtypeStruct((D_MODEL, D_IN_WIDE), BF16),
        compiler_params=_cparams(("arbitrary",)),
        name="prep_w_in",
    )(w_t)


def kernel(x, attn_norm_w, w_in, diff_lambda, diff_subln_w, hgrn_lb_logits, hgrn_norm_w, dn_conv_w,
           dn_A_log, dn_dt_bias, dn_norm_w, w_out, ffn_norm_w, ffn_w_up, ffn_conv_w, ffn_conv_b,
           ffn_w_down, final_norm_w):
    batch, seq, _ = x.shape
    h = x.reshape(batch * seq, D_MODEL)
    dn_alog = jnp.repeat(dn_A_log, HEAD_DIM, axis=-1)
    dn_dtb = jnp.repeat(dn_dt_bias, HEAD_DIM, axis=-1)
    hg_nw = jnp.tile(hgrn_norm_w, (1, HEADS))
    dn_nw = jnp.tile(dn_norm_w, (1, HEADS))
    for l in range(DEPTH):
        lambda_init = 0.8 - 0.6 * math.exp(-0.3 * l)
        att, rest = _inproj(h, attn_norm_w[l][None], _prep_w_in(w_in, l))
        att_o = _attention(att, diff_lambda[l], diff_subln_w[l][None], batch, seq, lambda_init)
        hg_o = _hgrn(rest, hgrn_lb_logits, hg_nw[l][None], batch, seq, l)
        dn_o = _deltanet(rest, dn_conv_w[l], dn_alog[l][None], dn_dtb[l][None], dn_nw[l][None],
                         batch, seq)
        h = _mix_ffn(att_o, hg_o, dn_o, h, _cast_layer(w_out, l), ffn_norm_w[l][None],
                     _cast_layer(ffn_w_up, l), ffn_conv_w[l], ffn_conv_b[l][None],
                     _cast_layer(ffn_w_down, l), final_norm_w[None], seq, final=(l == DEPTH - 1))
    return h.reshape(batch, seq, D_MODEL)
```

```python
import functools
import math

import jax
import jax.numpy as jnp
from jax import lax
from jax.experimental import pallas as pl
from jax.experimental.pallas import tpu as pltpu

F32 = jnp.float32
BF16 = jnp.bfloat16

D_MODEL = 1024
DEPTH = 2
ATT_QK_DIM = 64
ATT_V_DIM = 128
ATT_HEADS = 4
ATT_WIDTH = 512
HEADS = 4
HEAD_DIM = 64
REC_WIDTH = HEADS * HEAD_DIM
DN_CONV = 4
DN_CONV_CH = 3 * REC_WIDTH
FFN_CONV = 3
D_FF = 2816
HG_CHUNK = 16
DN_CHUNK = 64
EPS = 1e-6
MASK_NEG = -1e30
F_FLOOR = 1e-30
LOG2E = math.log2(math.e)

ATT_COLS = 3 * ATT_WIDTH
HG_COLS = 4 * REC_WIDTH
DN_COLS = 4 * REC_WIDTH
AB_COLS = 2 * REC_WIDTH
REST_COLS = HG_COLS + DN_COLS + AB_COLS
D_IN = ATT_COLS + HG_COLS + DN_COLS + 2 * HEADS
D_IN_WIDE = ATT_COLS + REST_COLS

VMEM_LIMIT = 56 * 1024 * 1024
LANES = 128
HALO = 16
ROW_STRIDE = 4

def _cparams(sem):
    return pltpu.CompilerParams(dimension_semantics=sem, vmem_limit_bytes=VMEM_LIMIT)


def _const_spec(shape):
    return pl.BlockSpec(shape, lambda *_: (0,) * len(shape), pipeline_mode=pl.Buffered(1))


def _rms(x, w):
    return x * lax.rsqrt(jnp.mean(x * x, axis=-1, keepdims=True) + EPS) * w


def _sigmoid(x):
    return 1.0 / (1.0 + jnp.exp(-x))


def _silu(x):
    return x * _sigmoid(x)


def _dot(a, b):
    return jnp.dot(a, b, preferred_element_type=F32)


def _head_of(idx):
    return lax.shift_right_logical(idx, 6)


def _same_head(shape):
    r = lax.broadcasted_iota(jnp.int32, shape, 0)
    c = lax.broadcasted_iota(jnp.int32, shape, 1)
    return _head_of(r) == _head_of(c)


def _cumsum_rows(incl, x):
    hi = x.astype(BF16)
    r1 = x - hi.astype(F32)
    mid = r1.astype(BF16)
    lo = (r1 - mid.astype(F32)).astype(BF16)
    return _dot(incl, hi) + _dot(incl, mid) + _dot(incl, lo)


def _chunk_incl(rows, chunk_shift):
    r = lax.broadcasted_iota(jnp.int32, (rows, rows), 0)
    c = lax.broadcasted_iota(jnp.int32, (rows, rows), 1)
    same = lax.shift_right_logical(r, chunk_shift) == lax.shift_right_logical(c, chunk_shift)
    return jnp.where(same & (c <= r), 1.0, 0.0).astype(BF16)


def _inproj_body(x_ref, nw_ref, w_ref, att_ref, rest_ref):
    u = _rms(x_ref[...], nw_ref[...]).astype(BF16)
    step = 512
    for c0 in range(0, ATT_COLS, step):
        att_ref[:, c0:c0 + step] = _dot(u, w_ref[:, c0:c0 + step]).astype(BF16)
    for c0 in range(0, REST_COLS, step):
        rest_ref[:, c0:c0 + step] = _dot(u, w_ref[:, ATT_COLS + c0:ATT_COLS + c0 + step])


def _inproj(h, nw, w, tm=512):
    n = h.shape[0]
    return pl.pallas_call(
        _inproj_body,
        grid=(n // tm,),
        in_specs=[
            pl.BlockSpec((tm, D_MODEL), lambda i: (i, 0)),
            _const_spec((1, D_MODEL)),
            _const_spec((D_MODEL, D_IN_WIDE)),
        ],
        out_specs=[
            pl.BlockSpec((tm, ATT_COLS), lambda i: (i, 0)),
            pl.BlockSpec((tm, REST_COLS), lambda i: (i, 0)),
        ],
        out_shape=[
            jax.ShapeDtypeStruct((n, ATT_COLS), BF16),
            jax.ShapeDtypeStruct((n, REST_COLS), F32),
        ],
        compiler_params=_cparams(("arbitrary",)),
        name="inproj",
    )(h, nw, w)


def _attn_body(lamp_ref, subw_ref, q_ref, k_ref, v_ref, o_ref,
               vt_scr, qt_scr, sa_scr, sb_scr, ma_scr, mb_scr, pa_scr, pb_scr, m_scr, l_scr, al_scr,
               acc_scr, *, tq, tk, lambda_init):
    i = pl.program_id(2)
    nblk = vt_scr.shape[0] - 1

    @pl.when(i == 0)
    def _():
        for c in range(nblk):
            vt_scr[c] = v_ref[c * tk:(c + 1) * tk, :].astype(F32).T.astype(BF16)
        vt_scr[nblk] = jnp.zeros(vt_scr.shape[1:], BF16)
        pb_scr[...] = jnp.zeros(pb_scr.shape, BF16)

    qt = q_ref[...].astype(F32).T * LOG2E
    first = lax.broadcasted_iota(jnp.int32, qt.shape, 0) < ATT_QK_DIM
    qt_scr[:, :tq] = jnp.where(first, qt, 0.0).astype(BF16)
    qt_scr[:, tq:] = jnp.where(first, 0.0, qt).astype(BF16)
    m_scr[...] = jnp.full(m_scr.shape, MASK_NEG, F32)
    l_scr[...] = jnp.zeros(l_scr.shape, F32)
    al_scr[...] = jnp.ones(al_scr.shape, F32)
    acc_scr[...] = jnp.zeros(acc_scr.shape, F32)

    def put_scores(dst, j):
        kj = k_ref[pl.ds(pl.multiple_of(j * tk, tk), tk), :]
        s = _dot(kj, qt_scr[...])
        dst[0][...] = s
        dst[1][...] = jnp.max(s, axis=0, keepdims=True)

    def phase(j, s_cur, p_cur, s_nxt, p_prv, masked, prefetch=True):
        if prefetch:
            put_scores(s_nxt, j + 1)
        pv = _dot(vt_scr[jnp.where(j == 0, nblk, j - 1)], p_prv[...])
        acc_scr[...] = al_scr[...] * acc_scr[...] + pv
        s = s_cur[0][...]
        if masked:
            kpos = lax.broadcasted_iota(jnp.int32, s.shape, 0) + (j * tk - i * tq)
            qpos = lax.broadcasted_iota(jnp.int32, s.shape, 1)
            qpos = jnp.where(qpos >= tq, qpos - tq, qpos)
            s = jnp.where(kpos <= qpos, s, MASK_NEG)
            m_tile = jnp.max(s, axis=0, keepdims=True)
        else:
            m_tile = s_cur[1][...]
        m_old = m_scr[...]
        m_new = jnp.maximum(m_old, m_tile)
        alpha = jnp.exp2(m_old - m_new)
        p = jnp.exp2(s - m_new)
        l_scr[...] = alpha * l_scr[...] + jnp.sum(p, axis=0, keepdims=True)
        m_scr[...] = m_new
        p_cur[...] = p.astype(BF16)
        al_scr[...] = alpha

    def finish(j, p_cur):
        acc_scr[...] = al_scr[...] * acc_scr[...] + _dot(vt_scr[j], p_cur[...])

    sa = (sa_scr, ma_scr)
    sb = (sb_scr, mb_scr)
    put_scores(sa, 0)

    def pair(jj, carry):
        phase(2 * jj, sa, pa_scr, sb, pb_scr, False)
        phase(2 * jj + 1, sb, pb_scr, sa, pa_scr, False)
        return carry

    lax.fori_loop(0, lax.shift_right_logical(i, 1), pair, 0)

    @pl.when((i & 1) == 0)
    def _():
        phase(i, sa, pa_scr, sb, pb_scr, True, prefetch=False)
        finish(i, pa_scr)

    @pl.when((i & 1) == 1)
    def _():
        phase(i - 1, sa, pa_scr, sb, pb_scr, False)
        phase(i, sb, pb_scr, sa, pa_scr, True, prefetch=False)
        finish(i, pb_scr)

    acc = acc_scr[...]
    lp = lamp_ref[...]
    lam = (jnp.exp(jnp.sum(lp[0:1] * lp[1:2], axis=-1, keepdims=True))
           - jnp.exp(jnp.sum(lp[2:3] * lp[3:4], axis=-1, keepdims=True)) + lambda_init)
    rl = 1.0 / l_scr[...]
    ot = acc[:, :tq] * rl[:, :tq] - lam * (acc[:, tq:] * rl[:, tq:])
    ot = ot * (lax.rsqrt(jnp.mean(ot * ot, axis=0, keepdims=True) + EPS) * (1.0 - lambda_init))
    o_ref[...] = (ot.T * subw_ref[...]).astype(BF16)


def _attention(att, lamp, subw, batch, seq, lambda_init, tq=512):
    n = att.shape[0]
    nq = seq // tq
    tk = tq
    body = functools.partial(_attn_body, tq=tq, tk=tk, lambda_init=lambda_init)
    return pl.pallas_call(
        body,
        grid=(batch, ATT_HEADS, nq),
        in_specs=[
            pl.BlockSpec((4, ATT_QK_DIM), lambda b, h, i: (0, 0)),
            pl.BlockSpec((1, ATT_V_DIM), lambda b, h, i: (0, 0)),
            pl.BlockSpec((tq, LANES), lambda b, h, i: (b * nq + i, h)),
            pl.BlockSpec((seq, LANES), lambda b, h, i: (b, ATT_HEADS + h)),
            pl.BlockSpec((seq, LANES), lambda b, h, i: (b, 2 * ATT_HEADS + h)),
        ],
        out_specs=pl.BlockSpec((tq, LANES), lambda b, h, i: (b * nq + i, h)),
        out_shape=jax.ShapeDtypeStruct((n, ATT_WIDTH), BF16),
        scratch_shapes=[
            pltpu.VMEM((seq // tk + 1, LANES, tk), BF16),
            pltpu.VMEM((LANES, 2 * tq), BF16),
            pltpu.VMEM((tk, 2 * tq), F32),
            pltpu.VMEM((tk, 2 * tq), F32),
            pltpu.VMEM((1, 2 * tq), F32),
            pltpu.VMEM((1, 2 * tq), F32),
            pltpu.VMEM((tk, 2 * tq), BF16),
            pltpu.VMEM((tk, 2 * tq), BF16),
            pltpu.VMEM((1, 2 * tq), F32),
            pltpu.VMEM((1, 2 * tq), F32),
            pltpu.VMEM((1, 2 * tq), F32),
            pltpu.VMEM((LANES, 2 * tq), F32),
        ],
        compiler_params=_cparams(("arbitrary", "arbitrary", "arbitrary")),
        name="diff_attention",
    )(lamp, subw, att, att, att)


def _hgrn_body(x_ref, lbl_ref, nw_ref, o_ref, st_scr, b_scr, q_scr, k_scr, v_scr,
               qd_scr, oi_scr, c_scr, *, layer, rows):
    @pl.when(pl.program_id(1) == 0)
    def _():
        st_scr[...] = jnp.zeros(st_scr.shape, F32)

    lg = lbl_ref[...]
    e = jnp.exp(lg - jnp.max(lg, axis=0, keepdims=True))
    sm = e / jnp.sum(e, axis=0, keepdims=True)
    lb = jnp.zeros((1, REC_WIDTH), F32)
    for i in range(1, layer + 1):
        lb = lb + sm[i:i + 1]

    x = x_ref[...]
    q = x[:, 0:REC_WIDTH]
    fp = x[:, REC_WIDTH:2 * REC_WIDTH]
    iv = x[:, 2 * REC_WIDTH:3 * REC_WIDTH]
    gate = x[:, 3 * REC_WIDTH:4 * REC_WIDTH]
    qf = _silu(q)
    f = lb + (1.0 - lb) * _sigmoid(fp)
    logf = jnp.log2(jnp.maximum(f, F_FLOOR))
    kf = (1.0 - lb) * _sigmoid(-fp)

    b = _cumsum_rows(_chunk_incl(rows, 4), logf)
    def put(ref, val):
        ref[0] = val[:, :LANES]
        ref[1] = val[:, LANES:]

    def rows_at(ref, sl):
        return jnp.concatenate([ref[0, sl, :], ref[1, sl, :]], axis=1)

    def row_rep(ref, r):
        return rows_at(ref, pl.ds(r, HG_CHUNK, stride=0))

    put(b_scr, b)
    put(c_scr, jnp.log2(kf) - b)
    put(k_scr, kf)
    put(v_scr, iv)
    q_scr[...] = qf
    qd_scr[...] = qf * jnp.exp2(b)

    same = _same_head((REC_WIDTH, REC_WIDTH))
    ind = jnp.where(same, 1.0, 0.0).astype(BF16)
    trow = lax.broadcasted_iota(jnp.int32, (HG_CHUNK, REC_WIDTH), 0)

    def chunk(c, carry):
        base = c * HG_CHUNK
        sl = pl.ds(base, HG_CHUNK)
        b_c = rows_at(b_scr, sl)
        q_c = q_scr[sl, :]
        b_last = row_rep(b_scr, base + HG_CHUNK - 1)
        kd_c = rows_at(k_scr, sl) * jnp.exp2(b_last - b_c)
        st = st_scr[...]
        o_inter = lax.dot_general(qd_scr[sl, :].astype(BF16), st.astype(BF16),
                                  (((1,), (1,)), ((), ())), preferred_element_type=F32)
        kvt = lax.dot_general(rows_at(v_scr, sl).astype(BF16), kd_c.astype(BF16),
                              (((0,), (0,)), ((), ())), preferred_element_type=F32)
        st_scr[...] = st * jnp.exp2(b_last[0:1]) + jnp.where(same, kvt, 0.0)
        slabs = []
        for s in range(HG_CHUNK):
            rel = jnp.where(trow >= s, b_c + row_rep(c_scr, base + s), MASK_NEG)
            slabs.append((jnp.exp2(rel) * q_c).astype(BF16))
        a = _dot(jnp.concatenate(slabs, axis=0), ind)
        o_intra = a[0:HG_CHUNK] * row_rep(v_scr, base)
        for s in range(1, HG_CHUNK):
            o_intra = o_intra + a[s * HG_CHUNK:(s + 1) * HG_CHUNK] * row_rep(v_scr, base + s)
        oi_scr[sl, :] = o_inter + o_intra
        return carry

    for c in range(rows // HG_CHUNK):
        chunk(c, 0)

    o = oi_scr[...]
    ms = _dot((o * o).astype(BF16), ind) * (1.0 / HEAD_DIM)
    y = o * lax.rsqrt(ms + EPS) * nw_ref[...] * _silu(gate)
    o_ref[...] = y.astype(BF16)


def _hgrn(rest, lb_logits, nw, batch, seq, layer, rows=256):
    n = rest.shape[0]
    nb = seq // rows
    body = functools.partial(_hgrn_body, layer=layer, rows=rows)
    rec = pltpu.VMEM((rows, REC_WIDTH), F32)
    slab = pltpu.VMEM((REC_WIDTH // LANES, rows, LANES), F32)
    return pl.pallas_call(
        body,
        grid=(batch, nb),
        in_specs=[
            pl.BlockSpec((rows, HG_COLS), lambda b, r: (b * nb + r, 0)),
            pl.BlockSpec((DEPTH, REC_WIDTH), lambda b, r: (0, 0)),
            pl.BlockSpec((1, REC_WIDTH), lambda b, r: (0, 0)),
        ],
        out_specs=pl.BlockSpec((rows, REC_WIDTH), lambda b, r: (b * nb + r, 0)),
        out_shape=jax.ShapeDtypeStruct((n, REC_WIDTH), BF16),
        scratch_shapes=[pltpu.VMEM((REC_WIDTH, REC_WIDTH), F32),
                        slab, rec, slab, slab, rec, rec, slab],
        compiler_params=_cparams(("arbitrary", "arbitrary")),
        name="hgrn2",
    )(rest, lb_logits, nw)


def _dn_body(x_ref, a_ref, b_ref, cw_ref, alog_ref, dtb_ref, nw_ref, o_ref,
             carry_scr, xs_scr, ys_scr, s_scr, *, rows):
    nbatch = x_ref.shape[0]
    nchunk = rows // DN_CHUNK
    flat = nbatch * rows

    @pl.when(pl.program_id(0) == 0)
    def _():
        carry_scr[...] = jnp.zeros(carry_scr.shape, F32)
        s_scr[...] = jnp.zeros(s_scr.shape, F32)

    cw = cw_ref[...]
    quarter = rows // ROW_STRIDE
    ys = []
    for bi in range(nbatch):
        for s in range(DN_CONV_CH // LANES):
            lanes = slice(s * LANES, (s + 1) * LANES)
            xs_scr[bi, s, 0:8, :] = carry_scr[bi, s]
            xs_scr[bi, s, 8:, :] = x_ref[bi, :, lanes]
            carry_scr[bi, s] = x_ref[bi, rows - 8:rows, lanes]
            for r in range(ROW_STRIDE):
                acc = cw[0:1, lanes] * xs_scr[bi, s, pl.ds(5 + r, quarter, stride=ROW_STRIDE), :]
                for j in range(1, DN_CONV):
                    acc = acc + cw[j:j + 1, lanes] * xs_scr[bi, s, pl.ds(5 + j + r, quarter,
                                                                         stride=ROW_STRIDE), :]
                ys_scr[bi, s, pl.ds(r, quarter, stride=ROW_STRIDE), :] = _silu(acc)
        ys.append(jnp.concatenate([ys_scr[bi, s] for s in range(DN_CONV_CH // LANES)], axis=1))
    y = jnp.concatenate(ys, axis=0)
    z = x_ref[...].reshape(flat, DN_COLS)[:, DN_CONV_CH:]
    qc = y[:, 0:REC_WIDTH]
    kc = y[:, REC_WIDTH:2 * REC_WIDTH]
    vc = y[:, 2 * REC_WIDTH:]

    same = _same_head((REC_WIDTH, REC_WIDTH))
    ind = jnp.where(same, 1.0, 0.0).astype(BF16)
    qn = qc * lax.rsqrt(_dot((qc * qc).astype(BF16), ind) + EPS) * (HEAD_DIM ** -0.5)
    kn = kc * lax.rsqrt(_dot((kc * kc).astype(BF16), ind) + EPS)

    sp_in = a_ref[...].reshape(flat, REC_WIDTH) + dtb_ref[...]
    softplus = jnp.maximum(sp_in, 0.0) + jnp.log1p(jnp.exp(-jnp.abs(sp_in)))
    g = -jnp.exp(alog_ref[...]) * softplus
    beta = _sigmoid(b_ref[...].reshape(flat, REC_WIDTH))
    kb = kn * beta
    vb = vc * beta

    crow = lax.broadcasted_iota(jnp.int32, (DN_CHUNK, REC_WIDTH), 0)
    slane = lax.broadcasted_iota(jnp.int32, (DN_CHUNK, REC_WIDTH), 1) & (HEAD_DIM - 1)
    diag = crow == slane
    eye = jnp.where(diag, 1.0, 0.0).astype(F32)
    incl = _chunk_incl(DN_CHUNK, 6)

    def bdiag(v):
        v16 = v.astype(BF16)
        return jnp.where(same, jnp.concatenate([v16] * HEADS, axis=0), jnp.zeros((), BF16))

    chunks = [(bi, n) for n in range(nchunk) for bi in range(nbatch)]
    rows_of = {}
    for bi, n in chunks:
        start = bi * rows + n * DN_CHUNK
        rows_of[(bi, n)] = slice(start, start + DN_CHUNK)
    gc, g_row, kt_bd, a_kk, a_qk, xinv, pw, u, w = {}, {}, {}, {}, {}, {}, {}, {}, {}
    for ch in chunks:
        gc[ch] = _cumsum_rows(incl, g[rows_of[ch]])
    for ch in chunks:
        g_row[ch] = jnp.sum(jnp.where(diag, gc[ch], 0.0), axis=0, keepdims=True)
        kt = jnp.concatenate([kn[rows_of[ch]]] * HEADS, axis=0).T
        kt_bd[ch] = jnp.where(same, kt, 0.0)
    for ch in chunks:
        sl = rows_of[ch]
        decay = jnp.exp(jnp.where(slane <= crow, gc[ch] - g_row[ch], MASK_NEG))
        sc = _dot(jnp.concatenate([kb[sl], qn[sl]], axis=0).astype(BF16), kt_bd[ch].astype(BF16))
        a_kk[ch] = jnp.where(slane < crow, sc[:DN_CHUNK] * decay, 0.0)
        a_qk[ch] = sc[DN_CHUNK:] * decay
    for ch in chunks:
        xinv[ch] = eye - a_kk[ch]
        pw[ch] = _dot(a_kk[ch].astype(BF16), bdiag(a_kk[ch]))
    for _ in range(4):
        for ch in chunks:
            r = _dot(jnp.concatenate([xinv[ch], pw[ch]], axis=0).astype(BF16), bdiag(pw[ch]))
            xinv[ch] = xinv[ch] + r[:DN_CHUNK]
            pw[ch] = r[DN_CHUNK:]
    for ch in chunks:
        sl = rows_of[ch]
        xi = (xinv[ch] + _dot(xinv[ch].astype(BF16), bdiag(pw[ch]))).astype(BF16)
        u[ch] = _dot(xi, bdiag(vb[sl]))
        w[ch] = _dot(xi, bdiag(kb[sl] * jnp.exp(gc[ch])))
    outs = {}
    state = [s_scr[bi] for bi in range(nbatch)]
    for n in range(nchunk):
        for bi in range(nbatch):
            ch = (bi, n)
            q_dec = qn[rows_of[ch]] * jnp.exp(gc[ch])
            r1 = _dot(jnp.concatenate([w[ch], q_dec], axis=0).astype(BF16), state[bi].astype(BF16))
            v_new = u[ch] - r1[:DN_CHUNK]
            gl_row = gc[ch][DN_CHUNK - 1:DN_CHUNK]
            kdt_bd = kt_bd[ch] * jnp.exp(gl_row - g_row[ch])
            r2 = _dot(jnp.concatenate([a_qk[ch], kdt_bd], axis=0).astype(BF16), bdiag(v_new))
            outs[ch] = r1[DN_CHUNK:] + r2[:DN_CHUNK]
            state[bi] = state[bi] * jnp.exp(gl_row) + r2[DN_CHUNK:]
    for bi in range(nbatch):
        s_scr[bi] = state[bi]

    o = jnp.concatenate([outs[(bi, n)] for bi in range(nbatch) for n in range(nchunk)], axis=0)
    ms = _dot((o * o).astype(BF16), ind) * (1.0 / HEAD_DIM)
    o = o * lax.rsqrt(ms + EPS) * nw_ref[...] * _silu(z)
    o_ref[...] = o.reshape(nbatch, rows, REC_WIDTH).astype(BF16)


def _deltanet(rest, cw, alog, dtb, nw, batch, seq, rows=256):
    rest3 = rest.reshape(batch, seq, REST_COLS)
    body = functools.partial(_dn_body, rows=rows)
    ab0 = (HG_COLS + DN_COLS) // REC_WIDTH
    out = pl.pallas_call(
        body,
        grid=(seq // rows,),
        in_specs=[
            pl.BlockSpec((batch, rows, DN_COLS), lambda r: (0, r, HG_COLS // DN_COLS)),
            pl.BlockSpec((batch, rows, REC_WIDTH), lambda r: (0, r, ab0)),
            pl.BlockSpec((batch, rows, REC_WIDTH), lambda r: (0, r, ab0 + 1)),
            pl.BlockSpec((DN_CONV, DN_CONV_CH), lambda r: (0, 0)),
            pl.BlockSpec((1, REC_WIDTH), lambda r: (0, 0)),
            pl.BlockSpec((1, REC_WIDTH), lambda r: (0, 0)),
            pl.BlockSpec((1, REC_WIDTH), lambda r: (0, 0)),
        ],
        out_specs=pl.BlockSpec((batch, rows, REC_WIDTH), lambda r: (0, r, 0)),
        out_shape=jax.ShapeDtypeStruct((batch, seq, REC_WIDTH), BF16),
        scratch_shapes=[
            pltpu.VMEM((batch, DN_CONV_CH // LANES, 8, LANES), F32),
            pltpu.VMEM((batch, DN_CONV_CH // LANES, rows + 8, LANES), F32),
            pltpu.VMEM((batch, DN_CONV_CH // LANES, rows, LANES), F32),
            pltpu.VMEM((batch, REC_WIDTH, REC_WIDTH), F32),
        ],
        compiler_params=_cparams(("arbitrary",)),
        name="gated_deltanet",
    )(rest3, rest3, rest3, cw, alog, dtb, nw)
    return out.reshape(batch * seq, REC_WIDTH)


def _mix_ffn_body(att_ref, hg_ref, dn_ref, h_ref, wo_ref, nw_ref, wup_ref, cw_ref, cb_ref, wdn_ref,
                  fw_ref, o_ref, u_scr, act_scr, zg_scr, zv_scr, back_scr,
                  *, tm, tf, blocks_per_seq, final):
    first = pl.program_id(0) % blocks_per_seq == 0

    @pl.when(first)
    def _():
        u_scr[:HALO, :] = jnp.zeros((HALO, D_MODEL), BF16)

    @pl.when(jnp.logical_not(first))
    def _():
        u_scr[:HALO, :] = u_scr[tm:tm + HALO, :]

    h1 = h_ref[...] + _dot(att_ref[...], wo_ref[0:ATT_WIDTH, :])
    h1 = h1 + _dot(hg_ref[...], wo_ref[ATT_WIDTH:ATT_WIDTH + REC_WIDTH, :])
    h1 = h1 + _dot(dn_ref[...], wo_ref[ATT_WIDTH + REC_WIDTH:, :])
    o_ref[...] = h1
    u_scr[HALO:, :] = _rms(h1, nw_ref[...]).astype(BF16)
    u = u_scr[...]

    quarter = tm // ROW_STRIDE

    def park(z_scr, z):
        for s in range(tf // LANES):
            z_scr[s] = z[:, s * LANES:(s + 1) * LANES]

    def rows_back(z_scr, d):
        cols = []
        for s in range(tf // LANES):
            cols.append(jnp.concatenate(
                [z_scr[s, pl.ds(HALO + r - d, quarter, stride=ROW_STRIDE), :] for r in range(ROW_STRIDE)],
                axis=0))
        return jnp.concatenate(cols, axis=1)

    def conv(z_scr, w, b):
        out = b
        for j in range(FFN_CONV):
            out = out + w[j:j + 1] * rows_back(z_scr, FFN_CONV - 1 - j)
        return out

    for j in range(D_FF // tf):
        gs = slice(j * tf, (j + 1) * tf)
        vs = slice(D_FF + j * tf, D_FF + (j + 1) * tf)
        park(zg_scr, _dot(u, wup_ref[:, gs]))
        park(zv_scr, _dot(u, wup_ref[:, vs]))
        gate = conv(zg_scr, cw_ref[:, gs], cb_ref[:, gs])
        val = conv(zv_scr, cw_ref[:, vs], cb_ref[:, vs])
        act_scr[:, gs] = (_silu(gate) * val).astype(BF16)

    act = act_scr[...]
    nstep = 256
    for c0 in range(0, D_MODEL, nstep):
        y = _dot(act, wdn_ref[:, c0:c0 + nstep])
        for s in range(nstep // LANES):
            slab = c0 // LANES + s
            for r in range(ROW_STRIDE):
                back_scr[slab, pl.ds(r, quarter, stride=ROW_STRIDE), :] = (
                    y[r * quarter:(r + 1) * quarter, s * LANES:(s + 1) * LANES])
            lanes = slice(slab * LANES, (slab + 1) * LANES)
            o_ref[:, lanes] = o_ref[:, lanes] + back_scr[slab]
    if final:
        o_ref[...] = _rms(o_ref[...], fw_ref[...])


def _mix_ffn(att_o, hg_o, dn_o, h, wo, nw, wup, cw, cb, wdn, fw, seq, final, tm=512, tf=256):
    n = h.shape[0]
    body = functools.partial(_mix_ffn_body, tm=tm, tf=tf, blocks_per_seq=seq // tm, final=final)
    return pl.pallas_call(
        body,
        grid=(n // tm,),
        in_specs=[
            pl.BlockSpec((tm, ATT_WIDTH), lambda i: (i, 0)),
            pl.BlockSpec((tm, REC_WIDTH), lambda i: (i, 0)),
            pl.BlockSpec((tm, REC_WIDTH), lambda i: (i, 0)),
            pl.BlockSpec((tm, D_MODEL), lambda i: (i, 0)),
            _const_spec((D_MODEL, D_MODEL)),
            _const_spec((1, D_MODEL)),
            _const_spec((D_MODEL, 2 * D_FF)),
            _const_spec((FFN_CONV, 2 * D_FF)),
            _const_spec((1, 2 * D_FF)),
            _const_spec((D_FF, D_MODEL)),
            _const_spec((1, D_MODEL)),
        ],
        out_specs=pl.BlockSpec((tm, D_MODEL), lambda i: (i, 0)),
        out_shape=jax.ShapeDtypeStruct((n, D_MODEL), F32),
        scratch_shapes=[
            pltpu.VMEM((HALO + tm, D_MODEL), BF16),
            pltpu.VMEM((tm, D_FF), BF16),
            pltpu.VMEM((tf // LANES, HALO + tm, LANES), F32),
            pltpu.VMEM((tf // LANES, HALO + tm, LANES), F32),
            pltpu.VMEM((D_MODEL // LANES, tm, LANES), F32),
        ],
        compiler_params=_cparams(("arbitrary",)),
        name="mix_ffn",
    )(att_o, hg_o, dn_o, h, wo, nw, wup, cw, cb, wdn, fw)


def _cast_body(x_ref, o_ref):
    o_ref[...] = x_ref[...].astype(BF16)


CAST_BLOCK_BYTES = 8 * 1024 * 1024


def _cast_layer(w, layer):
    _, r, c = w.shape
    rb = r
    while rb * c * 4 > CAST_BLOCK_BYTES and rb % 32 == 0:
        rb //= 2
    return pl.pallas_call(
        _cast_body,
        grid=(r // rb,),
        in_specs=[pl.BlockSpec((None, rb, c), lambda i: (layer, i, 0))],
        out_specs=pl.BlockSpec((rb, c), lambda i: (i, 0)),
        out_shape=jax.ShapeDtypeStruct((r, c), BF16),
        compiler_params=_cparams(("arbitrary",)),
        name="cast_weight",
    )(w)


W_IN_STEP = 512


def _prep_w_in_body(x_ref, o_ref):
    j = pl.program_id(0)
    last = D_IN_WIDE // W_IN_STEP - 1

    @pl.when(j == 0)
    def _():
        o_ref[...] = (x_ref[...] * (ATT_QK_DIM ** -0.5)).T.astype(BF16)

    @pl.when(jnp.logical_and(j > 0, j < last))
    def _():
        o_ref[...] = x_ref[...].T.astype(BF16)

    @pl.when(j == last)
    def _():
        rep = [jnp.broadcast_to(x_ref[r:r + 1, :], (HEAD_DIM, D_MODEL)) for r in range(2 * HEADS)]
        o_ref[...] = jnp.concatenate(rep, axis=0).T.astype(BF16)


def _prep_w_in(w_in, layer):
    assert D_IN - 2 * HEADS == D_IN_WIDE - W_IN_STEP and AB_COLS == W_IN_STEP
    w_t = jnp.swapaxes(w_in, 1, 2)
    return pl.pallas_call(
        _prep_w_in_body,
        grid=(D_IN_WIDE // W_IN_STEP,),
        in_specs=[pl.BlockSpec((None, W_IN_STEP, D_MODEL), lambda j: (layer, j, 0))],
        out_specs=pl.BlockSpec((D_MODEL, W_IN_STEP), lambda j: (0, j)),
        out_shape=jax.ShapeDtypeStruct((D_MODEL, D_IN_WIDE), BF16),
        compiler_params=_cparams(("arbitrary",)),
        name="prep_w_in",
    )(w_t)


def kernel(x, attn_norm_w, w_in, diff_lambda, diff_subln_w, hgrn_lb_logits, hgrn_norm_w, dn_conv_w,
           dn_A_log, dn_dt_bias, dn_norm_w, w_out, ffn_norm_w, ffn_w_up, ffn_conv_w, ffn_conv_b,
           ffn_w_down, final_norm_w):
    batch, seq, _ = x.shape
    h = x.reshape(batch * seq, D_MODEL)
    dn_alog = jnp.repeat(dn_A_log, HEAD_DIM, axis=-1)
    dn_dtb = jnp.repeat(dn_dt_bias, HEAD_DIM, axis=-1)
    hg_nw = jnp.tile(hgrn_norm_w, (1, HEADS))
    dn_nw = jnp.tile(dn_norm_w, (1, HEADS))
    for l in range(DEPTH):
        lambda_init = 0.8 - 0.6 * math.exp(-0.3 * l)
        att, rest = _inproj(h, attn_norm_w[l][None], _prep_w_in(w_in, l))
        att_o = _attention(att, diff_lambda[l], diff_subln_w[l][None], batch, seq, lambda_init)
        hg_o = _hgrn(rest, hgrn_lb_logits, hg_nw[l][None], batch, seq, l)
        dn_o = _deltanet(rest, dn_conv_w[l], dn_alog[l][None], dn_dtb[l][None], dn_nw[l][None],
                         batch, seq)
        h = _mix_ffn(att_o, hg_o, dn_o, h, _cast_layer(w_out, l), ffn_norm_w[l][None],
                     _cast_layer(ffn_w_up, l), ffn_conv_w[l], ffn_conv_b[l][None],
                     _cast_layer(ffn_w_down, l), final_norm_w[None], seq, final=(l == DEPTH - 1))
    return h.reshape(batch, seq, D_MODEL)
```

```python
import functools
import math

import jax
import jax.numpy as jnp
from jax import lax
from jax.experimental import pallas as pl
from jax.experimental.pallas import tpu as pltpu

F32 = jnp.float32
BF16 = jnp.bfloat16

D_MODEL = 1024
DEPTH = 2
ATT_QK_DIM = 64
ATT_V_DIM = 128
ATT_HEADS = 4
ATT_WIDTH = 512
HEADS = 4
HEAD_DIM = 64
REC_WIDTH = HEADS * HEAD_DIM
DN_CONV = 4
DN_CONV_CH = 3 * REC_WIDTH
FFN_CONV = 3
D_FF = 2816
HG_CHUNK = 16
DN_CHUNK = 64
EPS = 1e-6
MASK_NEG = -1e30
F_FLOOR = 1e-30
LOG2E = math.log2(math.e)

ATT_COLS = 3 * ATT_WIDTH
HG_COLS = 4 * REC_WIDTH
DN_COLS = 4 * REC_WIDTH
AB_COLS = 2 * REC_WIDTH
REST_COLS = HG_COLS + DN_COLS + AB_COLS
D_IN = ATT_COLS + HG_COLS + DN_COLS + 2 * HEADS
D_IN_WIDE = ATT_COLS + REST_COLS

VMEM_LIMIT = 56 * 1024 * 1024
LANES = 128
HALO = 16
ROW_STRIDE = 4

def _cparams(sem):
    return pltpu.CompilerParams(dimension_semantics=sem, vmem_limit_bytes=VMEM_LIMIT)


def _const_spec(shape):
    return pl.BlockSpec(shape, lambda *_: (0,) * len(shape), pipeline_mode=pl.Buffered(1))


def _rms(x, w):
    return x * lax.rsqrt(jnp.mean(x * x, axis=-1, keepdims=True) + EPS) * w


def _sigmoid(x):
    return 1.0 / (1.0 + jnp.exp(-x))


def _silu(x):
    return x * _sigmoid(x)


def _dot(a, b):
    return jnp.dot(a, b, preferred_element_type=F32)


def _head_of(idx):
    return lax.shift_right_logical(idx, 6)


def _same_head(shape):
    r = lax.broadcasted_iota(jnp.int32, shape, 0)
    c = lax.broadcasted_iota(jnp.int32, shape, 1)
    return _head_of(r) == _head_of(c)


def _cumsum_rows(incl, x):
    hi = x.astype(BF16)
    r1 = x - hi.astype(F32)
    mid = r1.astype(BF16)
    lo = (r1 - mid.astype(F32)).astype(BF16)
    return _dot(incl, hi) + _dot(incl, mid) + _dot(incl, lo)


def _chunk_incl(rows, chunk_shift):
    r = lax.broadcasted_iota(jnp.int32, (rows, rows), 0)
    c = lax.broadcasted_iota(jnp.int32, (rows, rows), 1)
    same = lax.shift_right_logical(r, chunk_shift) == lax.shift_right_logical(c, chunk_shift)
    return jnp.where(same & (c <= r), 1.0, 0.0).astype(BF16)


def _inproj_body(x_ref, nw_ref, w_ref, att_ref, rest_ref):
    u = _rms(x_ref[...], nw_ref[...]).astype(BF16)
    step = 512
    for c0 in range(0, ATT_COLS, step):
        att_ref[:, c0:c0 + step] = _dot(u, w_ref[:, c0:c0 + step]).astype(BF16)
    for c0 in range(0, REST_COLS, step):
        rest_ref[:, c0:c0 + step] = _dot(u, w_ref[:, ATT_COLS + c0:ATT_COLS + c0 + step])


def _inproj(h, nw, w, tm=512):
    n = h.shape[0]
    return pl.pallas_call(
        _inproj_body,
        grid=(n // tm,),
        in_specs=[
            pl.BlockSpec((tm, D_MODEL), lambda i: (i, 0)),
            _const_spec((1, D_MODEL)),
            _const_spec((D_MODEL, D_IN_WIDE)),
        ],
        out_specs=[
            pl.BlockSpec((tm, ATT_COLS), lambda i: (i, 0)),
            pl.BlockSpec((tm, REST_COLS), lambda i: (i, 0)),
        ],
        out_shape=[
            jax.ShapeDtypeStruct((n, ATT_COLS), BF16),
            jax.ShapeDtypeStruct((n, REST_COLS), F32),
        ],
        compiler_params=_cparams(("arbitrary",)),
        name="inproj",
    )(h, nw, w)


def _attn_body(lamp_ref, subw_ref, q_ref, k_ref, v_ref, o_ref,
               vt_scr, qt_scr, sa_scr, sb_scr, ma_scr, mb_scr, pa_scr, pb_scr, m_scr, l_scr, al_scr,
               acc_scr, *, tq, tk, lambda_init):
    i = pl.program_id(2)
    nblk = vt_scr.shape[0] - 1

    @pl.when(i == 0)
    def _():
        for c in range(nblk):
            vt_scr[c] = v_ref[c * tk:(c + 1) * tk, :].astype(F32).T.astype(BF16)
        vt_scr[nblk] = jnp.zeros(vt_scr.shape[1:], BF16)
        pb_scr[...] = jnp.zeros(pb_scr.shape, BF16)

    qt = q_ref[...].astype(F32).T * LOG2E
    first = lax.broadcasted_iota(jnp.int32, qt.shape, 0) < ATT_QK_DIM
    qt_scr[:, :tq] = jnp.where(first, qt, 0.0).astype(BF16)
    qt_scr[:, tq:] = jnp.where(first, 0.0, qt).astype(BF16)
    m_scr[...] = jnp.full(m_scr.shape, MASK_NEG, F32)
    l_scr[...] = jnp.zeros(l_scr.shape, F32)
    al_scr[...] = jnp.ones(al_scr.shape, F32)
    acc_scr[...] = jnp.zeros(acc_scr.shape, F32)

    def put_scores(dst, j):
        kj = k_ref[pl.ds(pl.multiple_of(j * tk, tk), tk), :]
        s = _dot(kj, qt_scr[...])
        dst[0][...] = s
        dst[1][...] = jnp.max(s, axis=0, keepdims=True)

    def phase(j, s_cur, p_cur, s_nxt, p_prv, masked, prefetch=True):
        if prefetch:
            put_scores(s_nxt, j + 1)
        pv = _dot(vt_scr[jnp.where(j == 0, nblk, j - 1)], p_prv[...])
        acc_scr[...] = al_scr[...] * acc_scr[...] + pv
        s = s_cur[0][...]
        if masked:
            kpos = lax.broadcasted_iota(jnp.int32, s.shape, 0) + (j * tk - i * tq)
            qpos = lax.broadcasted_iota(jnp.int32, s.shape, 1)
            qpos = jnp.where(qpos >= tq, qpos - tq, qpos)
            s = jnp.where(kpos <= qpos, s, MASK_NEG)
            m_tile = jnp.max(s, axis=0, keepdims=True)
        else:
            m_tile = s_cur[1][...]
        m_old = m_scr[...]
        m_new = jnp.maximum(m_old, m_tile)
        alpha = jnp.exp2(m_old - m_new)
        p = jnp.exp2(s - m_new)
        l_scr[...] = alpha * l_scr[...] + jnp.sum(p, axis=0, keepdims=True)
        m_scr[...] = m_new
        p_cur[...] = p.astype(BF16)
        al_scr[...] = alpha

    def finish(j, p_cur):
        acc_scr[...] = al_scr[...] * acc_scr[...] + _dot(vt_scr[j], p_cur[...])

    sa = (sa_scr, ma_scr)
    sb = (sb_scr, mb_scr)
    put_scores(sa, 0)

    def pair(jj, carry):
        phase(2 * jj, sa, pa_scr, sb, pb_scr, False)
        phase(2 * jj + 1, sb, pb_scr, sa, pa_scr, False)
        return carry

    lax.fori_loop(0, lax.shift_right_logical(i, 1), pair, 0)

    @pl.when((i & 1) == 0)
    def _():
        phase(i, sa, pa_scr, sb, pb_scr, True, prefetch=False)
        finish(i, pa_scr)

    @pl.when((i & 1) == 1)
    def _():
        phase(i - 1, sa, pa_scr, sb, pb_scr, False)
        phase(i, sb, pb_scr, sa, pa_scr, True, prefetch=False)
        finish(i, pb_scr)

    acc = acc_scr[...]
    lp = lamp_ref[...]
    lam = (jnp.exp(jnp.sum(lp[0:1] * lp[1:2], axis=-1, keepdims=True))
           - jnp.exp(jnp.sum(lp[2:3] * lp[3:4], axis=-1, keepdims=True)) + lambda_init)
    rl = 1.0 / l_scr[...]
    ot = acc[:, :tq] * rl[:, :tq] - lam * (acc[:, tq:] * rl[:, tq:])
    ot = ot * (lax.rsqrt(jnp.mean(ot * ot, axis=0, keepdims=True) + EPS) * (1.0 - lambda_init))
    o_ref[...] = (ot.T * subw_ref[...]).astype(BF16)


def _attention(att, lamp, subw, batch, seq, lambda_init, tq=512):
    n = att.shape[0]
    nq = seq // tq
    tk = tq
    body = functools.partial(_attn_body, tq=tq, tk=tk, lambda_init=lambda_init)
    return pl.pallas_call(
        body,
        grid=(batch, ATT_HEADS, nq),
        in_specs=[
            pl.BlockSpec((4, ATT_QK_DIM), lambda b, h, i: (0, 0)),
            pl.BlockSpec((1, ATT_V_DIM), lambda b, h, i: (0, 0)),
            pl.BlockSpec((tq, LANES), lambda b, h, i: (b * nq + i, h)),
            pl.BlockSpec((seq, LANES), lambda b, h, i: (b, ATT_HEADS + h)),
            pl.BlockSpec((seq, LANES), lambda b, h, i: (b, 2 * ATT_HEADS + h)),
        ],
        out_specs=pl.BlockSpec((tq, LANES), lambda b, h, i: (b * nq + i, h)),
        out_shape=jax.ShapeDtypeStruct((n, ATT_WIDTH), BF16),
        scratch_shapes=[
            pltpu.VMEM((seq // tk + 1, LANES, tk), BF16),
            pltpu.VMEM((LANES, 2 * tq), BF16),
            pltpu.VMEM((tk, 2 * tq), F32),
            pltpu.VMEM((tk, 2 * tq), F32),
            pltpu.VMEM((1, 2 * tq), F32),
            pltpu.VMEM((1, 2 * tq), F32),
            pltpu.VMEM((tk, 2 * tq), BF16),
            pltpu.VMEM((tk, 2 * tq), BF16),
            pltpu.VMEM((1, 2 * tq), F32),
            pltpu.VMEM((1, 2 * tq), F32),
            pltpu.VMEM((1, 2 * tq), F32),
            pltpu.VMEM((LANES, 2 * tq), F32),
        ],
        compiler_params=_cparams(("arbitrary", "arbitrary", "arbitrary")),
        name="diff_attention",
    )(lamp, subw, att, att, att)


def _hgrn_body(x_ref, lbl_ref, nw_ref, o_ref, st_scr, b_scr, q_scr, k_scr, v_scr,
               qd_scr, oi_scr, c_scr, *, layer, rows):
    @pl.when(pl.program_id(1) == 0)
    def _():
        st_scr[...] = jnp.zeros(st_scr.shape, F32)

    lg = lbl_ref[...]
    e = jnp.exp(lg - jnp.max(lg, axis=0, keepdims=True))
    sm = e / jnp.sum(e, axis=0, keepdims=True)
    lb = jnp.zeros((1, REC_WIDTH), F32)
    for i in range(1, layer + 1):
        lb = lb + sm[i:i + 1]

    x = x_ref[...]
    q = x[:, 0:REC_WIDTH]
    fp = x[:, REC_WIDTH:2 * REC_WIDTH]
    iv = x[:, 2 * REC_WIDTH:3 * REC_WIDTH]
    gate = x[:, 3 * REC_WIDTH:4 * REC_WIDTH]
    qf = _silu(q)
    f = lb + (1.0 - lb) * _sigmoid(fp)
    logf = jnp.log2(jnp.maximum(f, F_FLOOR))
    kf = (1.0 - lb) * _sigmoid(-fp)

    b = _cumsum_rows(_chunk_incl(rows, 4), logf)
    def put(ref, val):
        ref[0] = val[:, :LANES]
        ref[1] = val[:, LANES:]

    def rows_at(ref, sl):
        return jnp.concatenate([ref[0, sl, :], ref[1, sl, :]], axis=1)

    def row_rep(ref, r):
        return rows_at(ref, pl.ds(r, HG_CHUNK, stride=0))

    put(b_scr, b)
    put(c_scr, jnp.log2(kf) - b)
    put(k_scr, kf)
    put(v_scr, iv)
    q_scr[...] = qf
    qd_scr[...] = qf * jnp.exp2(b)

    same = _same_head((REC_WIDTH, REC_WIDTH))
    ind = jnp.where(same, 1.0, 0.0).astype(BF16)
    trow = lax.broadcasted_iota(jnp.int32, (HG_CHUNK, REC_WIDTH), 0)

    def chunk(c, carry):
        base = c * HG_CHUNK
        sl = pl.ds(base, HG_CHUNK)
        b_c = rows_at(b_scr, sl)
        q_c = q_scr[sl, :]
        b_last = row_rep(b_scr, base + HG_CHUNK - 1)
        kd_c = rows_at(k_scr, sl) * jnp.exp2(b_last - b_c)
        st = st_scr[...]
        o_inter = lax.dot_general(qd_scr[sl, :].astype(BF16), st.astype(BF16),
                                  (((1,), (1,)), ((), ())), preferred_element_type=F32)
        kvt = lax.dot_general(rows_at(v_scr, sl).astype(BF16), kd_c.astype(BF16),
                              (((0,), (0,)), ((), ())), preferred_element_type=F32)
        st_scr[...] = st * jnp.exp2(b_last[0:1]) + jnp.where(same, kvt, 0.0)
        slabs = []
        for s in range(HG_CHUNK):
            rel = jnp.where(trow >= s, b_c + row_rep(c_scr, base + s), MASK_NEG)
            slabs.append((jnp.exp2(rel) * q_c).astype(BF16))
        a = _dot(jnp.concatenate(slabs, axis=0), ind)
        o_intra = a[0:HG_CHUNK] * row_rep(v_scr, base)
        for s in range(1, HG_CHUNK):
            o_intra = o_intra + a[s * HG_CHUNK:(s + 1) * HG_CHUNK] * row_rep(v_scr, base + s)
        oi_scr[sl, :] = o_inter + o_intra
        return carry

    for c in range(rows // HG_CHUNK):
        chunk(c, 0)

    o = oi_scr[...]
    ms = _dot((o * o).astype(BF16), ind) * (1.0 / HEAD_DIM)
    y = o * lax.rsqrt(ms + EPS) * nw_ref[...] * _silu(gate)
    o_ref[...] = y.astype(BF16)


def _hgrn(rest, lb_logits, nw, batch, seq, layer, rows=256):
    n = rest.shape[0]
    nb = seq // rows
    body = functools.partial(_hgrn_body, layer=layer, rows=rows)
    rec = pltpu.VMEM((rows, REC_WIDTH), F32)
    slab = pltpu.VMEM((REC_WIDTH // LANES, rows, LANES), F32)
    return pl.pallas_call(
        body,
        grid=(batch, nb),
        in_specs=[
            pl.BlockSpec((rows, HG_COLS), lambda b, r: (b * nb + r, 0)),
            pl.BlockSpec((DEPTH, REC_WIDTH), lambda b, r: (0, 0)),
            pl.BlockSpec((1, REC_WIDTH), lambda b, r: (0, 0)),
        ],
        out_specs=pl.BlockSpec((rows, REC_WIDTH), lambda b, r: (b * nb + r, 0)),
        out_shape=jax.ShapeDtypeStruct((n, REC_WIDTH), BF16),
        scratch_shapes=[pltpu.VMEM((REC_WIDTH, REC_WIDTH), F32),
                        slab, rec, slab, slab, rec, rec, slab],
        compiler_params=_cparams(("arbitrary", "arbitrary")),
        name="hgrn2",
    )(rest, lb_logits, nw)


def _dn_body(x_ref, a_ref, b_ref, cw_ref, alog_ref, dtb_ref, nw_ref, o_ref,
             carry_scr, xs_scr, ys_scr, s_scr, *, rows):
    nbatch = x_ref.shape[0]
    nchunk = rows // DN_CHUNK
    flat = nbatch * rows

    @pl.when(pl.program_id(0) == 0)
    def _():
        carry_scr[...] = jnp.zeros(carry_scr.shape, F32)
        s_scr[...] = jnp.zeros(s_scr.shape, F32)

    cw = cw_ref[...]
    quarter = rows // ROW_STRIDE
    ys = []
    for bi in range(nbatch):
        for s in range(DN_CONV_CH // LANES):
            lanes = slice(s * LANES, (s + 1) * LANES)
            xs_scr[bi, s, 0:8, :] = carry_scr[bi, s]
            xs_scr[bi, s, 8:, :] = x_ref[bi, :, lanes]
            carry_scr[bi, s] = x_ref[bi, rows - 8:rows, lanes]
            for r in range(ROW_STRIDE):
                acc = cw[0:1, lanes] * xs_scr[bi, s, pl.ds(5 + r, quarter, stride=ROW_STRIDE), :]
                for j in range(1, DN_CONV):
                    acc = acc + cw[j:j + 1, lanes] * xs_scr[bi, s, pl.ds(5 + j + r, quarter,
                                                                         stride=ROW_STRIDE), :]
                ys_scr[bi, s, pl.ds(r, quarter, stride=ROW_STRIDE), :] = _silu(acc)
        ys.append(jnp.concatenate([ys_scr[bi, s] for s in range(DN_CONV_CH // LANES)], axis=1))
    y = jnp.concatenate(ys, axis=0)
    z = x_ref[...].reshape(flat, DN_COLS)[:, DN_CONV_CH:]
    qc = y[:, 0:REC_WIDTH]
    kc = y[:, REC_WIDTH:2 * REC_WIDTH]
    vc = y[:, 2 * REC_WIDTH:]

    same = _same_head((REC_WIDTH, REC_WIDTH))
    ind = jnp.where(same, 1.0, 0.0).astype(BF16)
    qn = qc * lax.rsqrt(_dot((qc * qc).astype(BF16), ind) + EPS) * (HEAD_DIM ** -0.5)
    kn = kc * lax.rsqrt(_dot((kc * kc).astype(BF16), ind) + EPS)

    sp_in = a_ref[...].reshape(flat, REC_WIDTH) + dtb_ref[...]
    softplus = jnp.maximum(sp_in, 0.0) + jnp.log1p(jnp.exp(-jnp.abs(sp_in)))
    g = -jnp.exp(alog_ref[...]) * softplus
    beta = _sigmoid(b_ref[...].reshape(flat, REC_WIDTH))
    kb = kn * beta
    vb = vc * beta

    crow = lax.broadcasted_iota(jnp.int32, (DN_CHUNK, REC_WIDTH), 0)
    slane = lax.broadcasted_iota(jnp.int32, (DN_CHUNK, REC_WIDTH), 1) & (HEAD_DIM - 1)
    diag = crow == slane
    eye = jnp.where(diag, 1.0, 0.0).astype(F32)
    incl = _chunk_incl(DN_CHUNK, 6)

    def bdiag(v):
        v16 = v.astype(BF16)
        return jnp.where(same, jnp.concatenate([v16] * HEADS, axis=0), jnp.zeros((), BF16))

    chunks = [(bi, n) for n in range(nchunk) for bi in range(nbatch)]
    rows_of = {}
    for bi, n in chunks:
        start = bi * rows + n * DN_CHUNK
        rows_of[(bi, n)] = slice(start, start + DN_CHUNK)
    gc, g_row, kt_bd, a_kk, a_qk, xinv, pw, u, w = {}, {}, {}, {}, {}, {}, {}, {}, {}
    for ch in chunks:
        gc[ch] = _cumsum_rows(incl, g[rows_of[ch]])
    for ch in chunks:
        g_row[ch] = jnp.sum(jnp.where(diag, gc[ch], 0.0), axis=0, keepdims=True)
        kt = jnp.concatenate([kn[rows_of[ch]]] * HEADS, axis=0).T
        kt_bd[ch] = jnp.where(same, kt, 0.0)
    for ch in chunks:
        sl = rows_of[ch]
        decay = jnp.exp(jnp.where(slane <= crow, gc[ch] - g_row[ch], MASK_NEG))
        sc = _dot(jnp.concatenate([kb[sl], qn[sl]], axis=0).astype(BF16), kt_bd[ch].astype(BF16))
        a_kk[ch] = jnp.where(slane < crow, sc[:DN_CHUNK] * decay, 0.0)
        a_qk[ch] = sc[DN_CHUNK:] * decay
    for ch in chunks:
        xinv[ch] = eye - a_kk[ch]
        pw[ch] = _dot(a_kk[ch].astype(BF16), bdiag(a_kk[ch]))
    for _ in range(4):
        for ch in chunks:
            r = _dot(jnp.concatenate([xinv[ch], pw[ch]], axis=0).astype(BF16), bdiag(pw[ch]))
            xinv[ch] = xinv[ch] + r[:DN_CHUNK]
            pw[ch] = r[DN_CHUNK:]
    for ch in chunks:
        sl = rows_of[ch]
        xi = (xinv[ch] + _dot(xinv[ch].astype(BF16), bdiag(pw[ch]))).astype(BF16)
        u[ch] = _dot(xi, bdiag(vb[sl]))
        w[ch] = _dot(xi, bdiag(kb[sl] * jnp.exp(gc[ch])))
    outs = {}
    state = [s_scr[bi] for bi in range(nbatch)]
    for n in range(nchunk):
        for bi in range(nbatch):
            ch = (bi, n)
            q_dec = qn[rows_of[ch]] * jnp.exp(gc[ch])
            r1 = _dot(jnp.concatenate([w[ch], q_dec], axis=0).astype(BF16), state[bi].astype(BF16))
            v_new = u[ch] - r1[:DN_CHUNK]
            gl_row = gc[ch][DN_CHUNK - 1:DN_CHUNK]
            kdt_bd = kt_bd[ch] * jnp.exp(gl_row - g_row[ch])
            r2 = _dot(jnp.concatenate([a_qk[ch], kdt_bd], axis=0).astype(BF16), bdiag(v_new))
            outs[ch] = r1[DN_CHUNK:] + r2[:DN_CHUNK]
            state[bi] = state[bi] * jnp.exp(gl_row) + r2[DN_CHUNK:]
    for bi in range(nbatch):
        s_scr[bi] = state[bi]

    o = jnp.concatenate([outs[(bi, n)] for bi in range(nbatch) for n in range(nchunk)], axis=0)
    ms = _dot((o * o).astype(BF16), ind) * (1.0 / HEAD_DIM)
    o = o * lax.rsqrt(ms + EPS) * nw_ref[...] * _silu(z)
    o_ref[...] = o.reshape(nbatch, rows, REC_WIDTH).astype(BF16)


def _deltanet(rest, cw, alog, dtb, nw, batch, seq, rows=256):
    rest3 = rest.reshape(batch, seq, REST_COLS)
    body = functools.partial(_dn_body, rows=rows)
    ab0 = (HG_COLS + DN_COLS) // REC_WIDTH
    out = pl.pallas_call(
        body,
        grid=(seq // rows,),
        in_specs=[
            pl.BlockSpec((batch, rows, DN_COLS), lambda r: (0, r, HG_COLS // DN_COLS)),
            pl.BlockSpec((batch, rows, REC_WIDTH), lambda r: (0, r, ab0)),
            pl.BlockSpec((batch, rows, REC_WIDTH), lambda r: (0, r, ab0 + 1)),
            pl.BlockSpec((DN_CONV, DN_CONV_CH), lambda r: (0, 0)),
            pl.BlockSpec((1, REC_WIDTH), lambda r: (0, 0)),
            pl.BlockSpec((1, REC_WIDTH), lambda r: (0, 0)),
            pl.BlockSpec((1, REC_WIDTH), lambda r: (0, 0)),
        ],
        out_specs=pl.BlockSpec((batch, rows, REC_WIDTH), lambda r: (0, r, 0)),
        out_shape=jax.ShapeDtypeStruct((batch, seq, REC_WIDTH), BF16),
        scratch_shapes=[
            pltpu.VMEM((batch, DN_CONV_CH // LANES, 8, LANES), F32),
            pltpu.VMEM((batch, DN_CONV_CH // LANES, rows + 8, LANES), F32),
            pltpu.VMEM((batch, DN_CONV_CH // LANES, rows, LANES), F32),
            pltpu.VMEM((batch, REC_WIDTH, REC_WIDTH), F32),
        ],
        compiler_params=_cparams(("arbitrary",)),
        name="gated_deltanet",
    )(rest3, rest3, rest3, cw, alog, dtb, nw)
    return out.reshape(batch * seq, REC_WIDTH)


def _mix_ffn_body(att_ref, hg_ref, dn_ref, h_ref, wo_ref, nw_ref, wup_ref, cw_ref, cb_ref, wdn_ref,
                  fw_ref, o_ref, u_scr, act_scr, zg_scr, zv_scr, back_scr,
                  *, tm, tf, blocks_per_seq, final):
    first = pl.program_id(0) % blocks_per_seq == 0

    @pl.when(first)
    def _():
        u_scr[:HALO, :] = jnp.zeros((HALO, D_MODEL), BF16)

    @pl.when(jnp.logical_not(first))
    def _():
        u_scr[:HALO, :] = u_scr[tm:tm + HALO, :]

    h1 = h_ref[...] + _dot(att_ref[...], wo_ref[0:ATT_WIDTH, :])
    h1 = h1 + _dot(hg_ref[...], wo_ref[ATT_WIDTH:ATT_WIDTH + REC_WIDTH, :])
    h1 = h1 + _dot(dn_ref[...], wo_ref[ATT_WIDTH + REC_WIDTH:, :])
    o_ref[...] = h1
    u_scr[HALO:, :] = _rms(h1, nw_ref[...]).astype(BF16)
    u = u_scr[...]

    quarter = tm // ROW_STRIDE

    def park(z_scr, z):
        for s in range(tf // LANES):
            z_scr[s] = z[:, s * LANES:(s + 1) * LANES]

    def rows_back(z_scr, d):
        cols = []
        for s in range(tf // LANES):
            cols.append(jnp.concatenate(
                [z_scr[s, pl.ds(HALO + r - d, quarter, stride=ROW_STRIDE), :] for r in range(ROW_STRIDE)],
                axis=0))
        return jnp.concatenate(cols, axis=1)

    def conv(z_scr, w, b):
        out = b
        for j in range(FFN_CONV):
            out = out + w[j:j + 1] * rows_back(z_scr, FFN_CONV - 1 - j)
        return out

    for j in range(D_FF // tf):
        gs = slice(j * tf, (j + 1) * tf)
        vs = slice(D_FF + j * tf, D_FF + (j + 1) * tf)
        park(zg_scr, _dot(u, wup_ref[:, gs]))
        park(zv_scr, _dot(u, wup_ref[:, vs]))
        gate = conv(zg_scr, cw_ref[:, gs], cb_ref[:, gs])
        val = conv(zv_scr, cw_ref[:, vs], cb_ref[:, vs])
        act_scr[:, gs] = (_silu(gate) * val).astype(BF16)

    act = act_scr[...]
    nstep = 256
    for c0 in range(0, D_MODEL, nstep):
        y = _dot(act, wdn_ref[:, c0:c0 + nstep])
        for s in range(nstep // LANES):
            slab = c0 // LANES + s
            for r in range(ROW_STRIDE):
                back_scr[slab, pl.ds(r, quarter, stride=ROW_STRIDE), :] = (
                    y[r * quarter:(r + 1) * quarter, s * LANES:(s + 1) * LANES])
            lanes = slice(slab * LANES, (slab + 1) * LANES)
            o_ref[:, lanes] = o_ref[:, lanes] + back_scr[slab]
    if final:
        o_ref[...] = _rms(o_ref[...], fw_ref[...])


def _mix_ffn(att_o, hg_o, dn_o, h, wo, nw, wup, cw, cb, wdn, fw, seq, layer, final, tm=512, tf=256):
    n = h.shape[0]
    body = functools.partial(_mix_ffn_body, tm=tm, tf=tf, blocks_per_seq=seq // tm, final=final)

    def layer_spec(r, c):
        return pl.BlockSpec((None, r, c), lambda i: (layer, 0, 0), pipeline_mode=pl.Buffered(1))

    return pl.pallas_call(
        body,
        grid=(n // tm,),
        in_specs=[
            pl.BlockSpec((tm, ATT_WIDTH), lambda i: (i, 0)),
            pl.BlockSpec((tm, REC_WIDTH), lambda i: (i, 0)),
            pl.BlockSpec((tm, REC_WIDTH), lambda i: (i, 0)),
            pl.BlockSpec((tm, D_MODEL), lambda i: (i, 0)),
            layer_spec(D_MODEL, D_MODEL),
            _const_spec((1, D_MODEL)),
            layer_spec(D_MODEL, 2 * D_FF),
            _const_spec((FFN_CONV, 2 * D_FF)),
            _const_spec((1, 2 * D_FF)),
            layer_spec(D_FF, D_MODEL),
            _const_spec((1, D_MODEL)),
        ],
        out_specs=pl.BlockSpec((tm, D_MODEL), lambda i: (i, 0)),
        out_shape=jax.ShapeDtypeStruct((n, D_MODEL), F32),
        scratch_shapes=[
            pltpu.VMEM((HALO + tm, D_MODEL), BF16),
            pltpu.VMEM((tm, D_FF), BF16),
            pltpu.VMEM((tf // LANES, HALO + tm, LANES), F32),
            pltpu.VMEM((tf // LANES, HALO + tm, LANES), F32),
            pltpu.VMEM((D_MODEL // LANES, tm, LANES), F32),
        ],
        compiler_params=_cparams(("arbitrary",)),
        name="mix_ffn",
    )(att_o, hg_o, dn_o, h, wo, nw, wup, cw, cb, wdn, fw)


def _cast_body(*refs):
    half = len(refs) // 2
    for x_ref, o_ref in zip(refs[:half], refs[half:]):
        o_ref[...] = x_ref[...].astype(BF16)


CAST_STEPS = 4


def _cast_weights(*ws):
    def spec(w):
        _, r, c = w.shape
        return pl.BlockSpec((None, r // CAST_STEPS, c), lambda g: (g // CAST_STEPS, g % CAST_STEPS, 0))

    return pl.pallas_call(
        _cast_body,
        grid=(DEPTH * CAST_STEPS,),
        in_specs=[spec(w) for w in ws],
        out_specs=[spec(w) for w in ws],
        out_shape=[jax.ShapeDtypeStruct(w.shape, BF16) for w in ws],
        compiler_params=_cparams(("arbitrary",)),
        name="cast_weights",
    )(*ws)


W_IN_STEP = 512


def _prep_w_in_body(x_ref, o_ref):
    j = pl.program_id(0)
    last = D_IN_WIDE // W_IN_STEP - 1

    @pl.when(j == 0)
    def _():
        o_ref[...] = (x_ref[...] * (ATT_QK_DIM ** -0.5)).T.astype(BF16)

    @pl.when(jnp.logical_and(j > 0, j < last))
    def _():
        o_ref[...] = x_ref[...].T.astype(BF16)

    @pl.when(j == last)
    def _():
        rep = [jnp.broadcast_to(x_ref[r:r + 1, :], (HEAD_DIM, D_MODEL)) for r in range(2 * HEADS)]
        o_ref[...] = jnp.concatenate(rep, axis=0).T.astype(BF16)


def _prep_w_in(w_in, layer):
    assert D_IN - 2 * HEADS == D_IN_WIDE - W_IN_STEP and AB_COLS == W_IN_STEP
    w_t = jnp.swapaxes(w_in, 1, 2)
    return pl.pallas_call(
        _prep_w_in_body,
        grid=(D_IN_WIDE // W_IN_STEP,),
        in_specs=[pl.BlockSpec((None, W_IN_STEP, D_MODEL), lambda j: (layer, j, 0))],
        out_specs=pl.BlockSpec((D_MODEL, W_IN_STEP), lambda j: (0, j)),
        out_shape=jax.ShapeDtypeStruct((D_MODEL, D_IN_WIDE), BF16),
        compiler_params=_cparams(("arbitrary",)),
        name="prep_w_in",
    )(w_t)


def kernel(x, attn_norm_w, w_in, diff_lambda, diff_subln_w, hgrn_lb_logits, hgrn_norm_w, dn_conv_w,
           dn_A_log, dn_dt_bias, dn_norm_w, w_out, ffn_norm_w, ffn_w_up, ffn_conv_w, ffn_conv_b,
           ffn_w_down, final_norm_w):
    batch, seq, _ = x.shape
    h = x.reshape(batch * seq, D_MODEL)
    dn_alog = jnp.repeat(dn_A_log, HEAD_DIM, axis=-1)
    dn_dtb = jnp.repeat(dn_dt_bias, HEAD_DIM, axis=-1)
    hg_nw = jnp.tile(hgrn_norm_w, (1, HEADS))
    dn_nw = jnp.tile(dn_norm_w, (1, HEADS))
    w_out_b, w_up_b, w_dn_b = _cast_weights(w_out, ffn_w_up, ffn_w_down)
    for l in range(DEPTH):
        lambda_init = 0.8 - 0.6 * math.exp(-0.3 * l)
        att, rest = _inproj(h, attn_norm_w[l][None], _prep_w_in(w_in, l))
        att_o = _attention(att, diff_lambda[l], diff_subln_w[l][None], batch, seq, lambda_init)
        hg_o = _hgrn(rest, hgrn_lb_logits, hg_nw[l][None], batch, seq, l)
        dn_o = _deltanet(rest, dn_conv_w[l], dn_alog[l][None], dn_dtb[l][None], dn_nw[l][None],
                         batch, seq)
        h = _mix_ffn(att_o, hg_o, dn_o, h, w_out_b, ffn_norm_w[l][None], w_up_b, ffn_conv_w[l],
                     ffn_conv_b[l][None], w_dn_b, final_norm_w[None], seq, l, final=(l == DEPTH - 1))
    return h.reshape(batch, seq, D_MODEL)
```
